```python
import math
import jax, jax.numpy as jnp
from jax import lax
import numpy as np

D_MODEL = 4096
BATCH = 1
SEQ = 8192
DEPTH = 4

HEAD_DIM = 128
ROT_DIM = HEAD_DIM // 4
ROPE_THETA = 500000.0
NORM_EPS = 1e-5
MIX_WIDTH = 3 * D_MODEL // 8
DIFF_HEADS = MIX_WIDTH // (2 * HEAD_DIM)
DIFF_QK = DIFF_HEADS * 2 * HEAD_DIM
DIFF_V = DIFF_HEADS * 2 * HEAD_DIM
DIFF_Q_BLOCK = 128
DIL_PATTERNS = ((128, 1), (512, 4), (2048, 16))
DIL_GROUPS = 3
DIL_HEADS_PER_GROUP = 4
DIL_HEADS = DIL_GROUPS * DIL_HEADS_PER_GROUP
DIL_WIDTH = DIL_HEADS * HEAD_DIM
DIL_OUT = DIL_HEADS_PER_GROUP * HEAD_DIM
DIL_BLOCK = 64
CONV_WIDTH = MIX_WIDTH
CONV_K = 3
POOL_WINDOWS = (2, 4, 8, 16)
POOL_WIDTH = MIX_WIDTH
POOL_GROUP = POOL_WIDTH // 4
N_BRANCHES = 4
GATE_RANK = D_MODEL // 8
D_FF = D_MODEL

kernel_name = "hybrid_parallel_gated_mixer_encoder"


def rmsnorm(x, g):
    x32 = x.astype(jnp.float32)
    y = x32 * lax.rsqrt(jnp.mean(x32 * x32, axis=-1, keepdims=True) + NORM_EPS)
    return (y * g.astype(jnp.float32)).astype(x.dtype)


def swiglu(h, wg, wu, wd):
    return (jax.nn.silu(h @ wg) * (h @ wu)) @ wd


def rope_tables(seq):
    inv = ROPE_THETA ** (-jnp.arange(0, ROT_DIM, 2, dtype=jnp.float32) / ROT_DIM)
    ang = jnp.arange(seq, dtype=jnp.float32)[:, None] * inv[None, :]
    return jnp.cos(ang), jnp.sin(ang)


def apply_partial_rope(t, cos, sin):
    shp = (1, cos.shape[0]) + (1,) * (t.ndim - 3) + (cos.shape[1],)
    c = cos.reshape(shp).astype(t.dtype)
    s = sin.reshape(shp).astype(t.dtype)
    half = ROT_DIM // 2
    t1, t2, rest = t[..., :half], t[..., half:ROT_DIM], t[..., ROT_DIM:]
    return jnp.concatenate([t1 * c - t2 * s, t2 * c + t1 * s, rest], axis=-1)


def diff_attention(q, k, v, lam, subln_g, lam_init):
    b, s, h = q.shape[:3]
    nq = s // DIFF_Q_BLOCK
    scale = HEAD_DIM ** -0.5
    qb = q.reshape(b, nq, DIFF_Q_BLOCK, h, 2, HEAD_DIM).transpose(1, 0, 2, 3, 4, 5)

    def block(qblk):
        sc = jnp.einsum('bqhcd,bkhcd->bchqk', qblk, k).astype(jnp.float32) * scale
        p = jax.nn.softmax(sc, axis=-1)
        a = (p[:, 0] - lam * p[:, 1]).astype(v.dtype)
        return jnp.einsum('bhqk,bkhe->bqhe', a, v)

    o = lax.map(block, qb)
    o = o.transpose(1, 0, 2, 3, 4).reshape(b, s, h, 2 * HEAD_DIM)
    o = rmsnorm(o, subln_g) * (1.0 - lam_init)
    return o.reshape(b, s, h * 2 * HEAD_DIM)


def banded_attention(q, k, v, radius):
    n, l, hd = q.shape
    blk = DIL_BLOCK
    nb = -(-l // blk)
    pad_q = nb * blk - l
    qp = jnp.pad(q, ((0, 0), (0, pad_q), (0, 0))).reshape(n, nb, blk, hd)

    def windows(t):
        tp = jnp.pad(t, ((0, 0), (blk, pad_q + blk), (0, 0))).reshape(n, nb + 2, blk, hd)
        return jnp.concatenate([tp[:, :-2], tp[:, 1:-1], tp[:, 2:]], axis=2)

    kw, vw = windows(k), windows(v)
    sc = jnp.einsum('nbqd,nbkd->nbqk', qp, kw).astype(jnp.float32) * (hd ** -0.5)
    qi = jnp.arange(nb)[:, None, None] * blk + jnp.arange(blk)[None, :, None]
    kj = (jnp.arange(nb)[:, None, None] - 1) * blk + jnp.arange(3 * blk)[None, None, :]
    valid = (jnp.abs(qi - kj) <= radius) & (kj >= 0) & (kj < l)
    sc = jnp.where(valid, sc, -1e30)
    m = jnp.max(sc, axis=-1, keepdims=True)
    p = jnp.exp(sc - m)
    den = jnp.sum(p, axis=-1, keepdims=True)
    o = jnp.einsum('nbqk,nbkd->nbqd', (p / den).astype(v.dtype), vw)
    lse = (m + jnp.log(den))[..., 0]
    return o.reshape(n, nb * blk, hd)[:, :l], lse.reshape(n, nb * blk)[:, :l]


def dilated_attention(q, k, v):
    b, s = q.shape[:2]
    hg = DIL_HEADS_PER_GROUP
    outs, lses = [], []
    for g, (window, dil) in enumerate(DIL_PATTERNS):
        radius = window // (2 * dil)
        lsub = s // dil

        def gather(t):
            t = t[:, :, g * hg:(g + 1) * hg].reshape(b, lsub, dil, hg, HEAD_DIM)
            return t.transpose(0, 2, 3, 1, 4).reshape(b * dil * hg, lsub, HEAD_DIM)

        o, lse = banded_attention(gather(q), gather(k), gather(v), radius)
        outs.append(o.reshape(b, dil, hg, lsub, HEAD_DIM).transpose(0, 3, 1, 2, 4).reshape(b, s, hg, HEAD_DIM))
        lses.append(lse.reshape(b, dil, hg, lsub).transpose(0, 3, 1, 2).reshape(b, s, hg))
    w = jax.nn.softmax(jnp.stack(lses, axis=0), axis=0)
    o = jnp.einsum('gbsh,gbshd->bshd', w.astype(q.dtype), jnp.stack(outs, axis=0))
    return o.reshape(b, s, DIL_OUT)


def short_conv_mixer(gate_b, gate_c, u, w_conv):
    z = gate_c * u
    y = lax.conv_general_dilated(z, w_conv[:, None, :].astype(z.dtype), window_strides=(1,),
                                 padding=((CONV_K // 2, CONV_K // 2),),
                                 dimension_numbers=('NWC', 'WIO', 'NWC'),
                                 feature_group_count=z.shape[-1])
    return gate_b * y


def pool_mixer(u, w_pool, pool_scale):
    b, s, _ = u.shape
    ug = u.reshape(b, s, 4, POOL_GROUP)
    csum = jnp.concatenate([jnp.zeros((b, 1, 4, POOL_GROUP), jnp.float32),
                            jnp.cumsum(ug.astype(jnp.float32), axis=1)], axis=1)
    pos = jnp.arange(s)
    pooled = []
    for gi, win in enumerate(POOL_WINDOWS):
        lo = jnp.clip(pos - win // 2, 0, s)
        hi = jnp.clip(pos - win // 2 + win, 0, s)
        cg = csum[:, :, gi]
        cnt = (hi - lo).astype(jnp.float32)[None, :, None]
        pooled.append((cg[:, hi] - cg[:, lo]) / cnt)
    pooled = jnp.stack(pooled, axis=2).astype(u.dtype) - ug
    y = jnp.einsum('bsgc,gcd->bsgd', pooled, w_pool)
    return y.reshape(b, s, POOL_WIDTH) * pool_scale


def setup_inputs(seed: int = 0) -> dict:
    key = jax.random.key(seed)
    k = jax.random.split(key, 27)
    L, D = DEPTH, D_MODEL

    def dense(kk, shape, fan_in):
        return jax.random.normal(kk, shape, jnp.float32) * fan_in ** -0.5

    def gain(kk, shape):
        return 1.0 + 0.02 * jax.random.normal(kk, shape, jnp.float32)

    in_cols = 2 * DIFF_QK + DIFF_V + 3 * DIL_WIDTH + 3 * CONV_WIDTH + POOL_WIDTH + GATE_RANK
    return {
        'x': jax.random.normal(k[0], (BATCH, SEQ, D), jnp.float32),
        'ln_ffn1': gain(k[1], (L, D)),
        'w1_gate': dense(k[2], (L, D, D_FF), D),
        'w1_up': dense(k[3], (L, D, D_FF), D),
        'w1_down': dense(k[4], (L, D_FF, D), D_FF),
        'ln_mix': gain(k[5], (L, D)),
        'w_in': dense(k[6], (L, D, in_cols), D),
        'lambda_q1': 0.1 * jax.random.normal(k[7], (L, HEAD_DIM), jnp.float32),
        'lambda_k1': 0.1 * jax.random.normal(k[8], (L, HEAD_DIM), jnp.float32),
        'lambda_q2': 0.1 * jax.random.normal(k[9], (L, HEAD_DIM), jnp.float32),
        'lambda_k2': 0.1 * jax.random.normal(k[10], (L, HEAD_DIM), jnp.float32),
        'subln': gain(k[11], (L, 2 * HEAD_DIM)),
        'conv_w': dense(k[12], (L, CONV_K, CONV_WIDTH), CONV_K),
        'pool_w': dense(k[13], (L, 4, POOL_GROUP, POOL_GROUP), POOL_GROUP),
        'pool_scale': 0.5 + 0.05 * jax.random.normal(k[14], (L, POOL_WIDTH), jnp.float32),
        'w_gate_up': dense(k[15], (L, GATE_RANK, N_BRANCHES * D), GATE_RANK),
        'b_gate': 0.02 * jax.random.normal(k[16], (L, N_BRANCHES * D), jnp.float32),
        'w_branch_a': dense(k[17], (L, DIFF_V, D), DIFF_V),
        'w_branch_b': dense(k[18], (L, DIL_OUT, D), DIL_OUT),
        'w_branch_c': dense(k[19], (L, CONV_WIDTH, D), CONV_WIDTH),
        'w_branch_d': dense(k[20], (L, POOL_WIDTH, D), POOL_WIDTH),
        'w_out': dense(k[21], (L, D, D), D),
        'ln_ffn2': gain(k[22], (L, D)),
        'w2_gate': dense(k[23], (L, D, D_FF), D),
        'w2_up': dense(k[24], (L, D, D_FF), D),
        'w2_down': dense(k[25], (L, D_FF, D), D_FF),
        'ln_final': gain(k[26], (D,)),
    }


def reference(x, ln_ffn1, w1_gate, w1_up, w1_down, ln_mix, w_in, lambda_q1, lambda_k1,
              lambda_q2, lambda_k2, subln, conv_w, pool_w, pool_scale, w_gate_up, b_gate,
              w_branch_a, w_branch_b, w_branch_c, w_branch_d, w_out, ln_ffn2, w2_gate,
              w2_up, w2_down, ln_final):
    b, s, d = x.shape
    cos, sin = rope_tables(s)
    sizes = (DIFF_QK, DIFF_QK, DIFF_V, DIL_WIDTH, DIL_WIDTH, DIL_WIDTH,
             CONV_WIDTH, CONV_WIDTH, CONV_WIDTH, POOL_WIDTH, GATE_RANK)
    split_idx = []
    acc = 0
    for sz in sizes[:-1]:
        acc += sz
        split_idx.append(acc)

    for l in range(DEPTH):
        x = x + 0.5 * swiglu(rmsnorm(x, ln_ffn1[l]), w1_gate[l], w1_up[l], w1_down[l])

        h = rmsnorm(x, ln_mix[l])
        proj = h @ w_in[l]
        qa, ka, va, qb, kb, vb, c_b, c_c, c_u, p_u, g_low = jnp.split(proj, split_idx, axis=-1)

        lam_init = 0.8 - 0.6 * math.exp(-0.3 * l)
        lam = (jnp.exp(jnp.sum(lambda_q1[l] * lambda_k1[l]).astype(jnp.float32))
               - jnp.exp(jnp.sum(lambda_q2[l] * lambda_k2[l]).astype(jnp.float32)) + lam_init)
        qa = apply_partial_rope(qa.reshape(b, s, DIFF_HEADS, 2, HEAD_DIM), cos, sin)
        ka = apply_partial_rope(ka.reshape(b, s, DIFF_HEADS, 2, HEAD_DIM), cos, sin)
        ya = diff_attention(qa, ka, va.reshape(b, s, DIFF_HEADS, 2 * HEAD_DIM), lam, subln[l], lam_init)

        qb = apply_partial_rope(qb.reshape(b, s, DIL_HEADS, HEAD_DIM), cos, sin)
        kb = apply_partial_rope(kb.reshape(b, s, DIL_HEADS, HEAD_DIM), cos, sin)
        yb = dilated_attention(qb, kb, vb.reshape(b, s, DIL_HEADS, HEAD_DIM))

        yc = short_conv_mixer(c_b, c_c, c_u, conv_w[l])

        yd = pool_mixer(p_u, pool_w[l], pool_scale[l])

        gates = jax.nn.sigmoid((g_low @ w_gate_up[l] + b_gate[l]).astype(jnp.float32))
        gates = gates.astype(x.dtype).reshape(b, s, N_BRANCHES, d)
        merged = (gates[:, :, 0] * (ya @ w_branch_a[l]) + gates[:, :, 1] * (yb @ w_branch_b[l])
                  + gates[:, :, 2] * (yc @ w_branch_c[l]) + gates[:, :, 3] * (yd @ w_branch_d[l]))
        x = x + merged @ w_out[l]

        x = x + 0.5 * swiglu(rmsnorm(x, ln_ffn2[l]), w2_gate[l], w2_up[l], w2_down[l])

    return rmsnorm(x, ln_final)
```

```python
import functools
import math

import jax
import jax.numpy as jnp
from jax import lax
from jax.experimental import pallas as pl
from jax.experimental.pallas import tpu as pltpu

F32 = jnp.float32
BF16 = jnp.bfloat16

HEAD_DIM = 128
ROT_DIM = HEAD_DIM // 4
ROPE_THETA = 500000.0
NORM_EPS = 1e-5
DIFF_HEADS = 6
DIL_PATTERNS = ((128, 1), (512, 4), (2048, 16))
DIL_HEADS_PER_GROUP = 4
POOL_WINDOWS = (2, 4, 8, 16)
N_BRANCHES = 4
HALO = 16
DIL_HALO = 64

VMEM_LIMIT_BYTES = 56 * 1024 * 1024


def _params(*sem):
    return pltpu.CompilerParams(dimension_semantics=sem, vmem_limit_bytes=VMEM_LIMIT_BYTES)


def _dot(a, b):
    return jnp.dot(a, b, preferred_element_type=F32)


def _dot_nt(a, b):
    return lax.dot_general(a, b, (((1,), (1,)), ((), ())), preferred_element_type=F32)


def _rmsnorm_body(x_ref, g_ref, o_ref):
    x = x_ref[...]
    ms = jnp.mean(x * x, axis=-1, keepdims=True)
    o_ref[...] = (x * lax.rsqrt(ms + NORM_EPS) * g_ref[...]).astype(o_ref.dtype)


def _rmsnorm(x, g, out_dtype, tm=256):
    s, d = x.shape
    return pl.pallas_call(
        _rmsnorm_body,
        grid=(s // tm,),
        in_specs=[pl.BlockSpec((tm, d), lambda i: (i, 0)),
                  pl.BlockSpec((1, d), lambda i: (0, 0))],
        out_specs=pl.BlockSpec((tm, d), lambda i: (i, 0)),
        out_shape=jax.ShapeDtypeStruct((s, d), out_dtype),
        compiler_params=_params("parallel"),
        name="rmsnorm",
    )(x, g.reshape(1, d))


def _ffn_up_body(h_ref, wg_ref, wu_ref, o_ref):
    h = h_ref[...]
    g = _dot(h, wg_ref[...])
    u = _dot(h, wu_ref[...])
    o_ref[...] = (g * jax.nn.sigmoid(g) * u).astype(o_ref.dtype)


def _ffn_up(h, wg, wu, tm=1024, tn=512):
    m, k = h.shape
    n = wg.shape[1]
    return pl.pallas_call(
        _ffn_up_body,
        grid=(m // tm, n // tn),
        in_specs=[pl.BlockSpec((tm, k), lambda i, j: (i, 0)),
                  pl.BlockSpec((k, tn), lambda i, j: (0, j)),
                  pl.BlockSpec((k, tn), lambda i, j: (0, j))],
        out_specs=pl.BlockSpec((tm, tn), lambda i, j: (i, j)),
        out_shape=jax.ShapeDtypeStruct((m, n), BF16),
        compiler_params=_params("parallel", "arbitrary"),
        name="ffn_up",
    )(h, wg, wu)


def _mm_residual_body(a_ref, w_ref, r_ref, o_ref, *, alpha):
    o_ref[...] = r_ref[...] + alpha * _dot(a_ref[...], w_ref[...])


def _mm_residual(a, w, res, alpha, tm=1024, tn=512):
    m, k = a.shape
    n = w.shape[1]
    return pl.pallas_call(
        functools.partial(_mm_residual_body, alpha=alpha),
        grid=(m // tm, n // tn),
        in_specs=[pl.BlockSpec((tm, k), lambda i, j: (i, 0)),
                  pl.BlockSpec((k, tn), lambda i, j: (0, j)),
                  pl.BlockSpec((tm, tn), lambda i, j: (i, j))],
        out_specs=pl.BlockSpec((tm, tn), lambda i, j: (i, j)),
        out_shape=jax.ShapeDtypeStruct((m, n), F32),
        compiler_params=_params("parallel", "arbitrary"),
        name="mm_residual",
    )(a, w, res)


def _mm_cols_body(a_ref, w_ref, o_ref):
    o_ref[...] = _dot(a_ref[...], w_ref[...]).astype(o_ref.dtype)


def _mm_cols(a, w, col_block_of, n_blocks, tm=1024, tn=512):
    m, k = a.shape
    return pl.pallas_call(
        _mm_cols_body,
        grid=(m // tm, n_blocks),
        in_specs=[pl.BlockSpec((tm, k), lambda i, j: (i, 0)),
                  pl.BlockSpec((k, tn), lambda i, j: (0, col_block_of(j)))],
        out_specs=pl.BlockSpec((tm, tn), lambda i, j: (i, j)),
        out_shape=jax.ShapeDtypeStruct((m, n_blocks * tn), BF16),
        compiler_params=_params("parallel", "arbitrary"),
        name="mm_cols",
    )(a, w)


def _mm_rope_body(a_ref, w_ref, c_ref, sa_ref, sb_ref, o_ref, *, q_scale, blocks_per_part):
    acc = _dot(a_ref[...], w_ref[...])
    tn = acc.shape[1]
    rep = tn // HEAD_DIM
    c = jnp.concatenate([c_ref[...]] * rep, axis=1)
    sa = jnp.concatenate([sa_ref[...]] * rep, axis=1)
    sb = jnp.concatenate([sb_ref[...]] * rep, axis=1)
    half = ROT_DIM // 2
    lo = pltpu.roll(acc, half, axis=1)
    hi = pltpu.roll(acc, tn - half, axis=1)
    out = acc * c + lo * sa + hi * sb
    j = pl.program_id(1)
    is_q = (j // blocks_per_part) % 2 == 0
    out = out * jnp.where(is_q, q_scale, 1.0)
    o_ref[...] = out.astype(o_ref.dtype)


def _mm_rope(a, w, tables, col_block_of, n_blocks, blocks_per_part, tm=1024, tn=512):
    m, k = a.shape
    c, sa, sb = tables
    tab_spec = pl.BlockSpec((tm, HEAD_DIM), lambda i, j: (i, 0))
    return pl.pallas_call(
        functools.partial(_mm_rope_body, q_scale=HEAD_DIM ** -0.5, blocks_per_part=blocks_per_part),
        grid=(m // tm, n_blocks),
        in_specs=[pl.BlockSpec((tm, k), lambda i, j: (i, 0)),
                  pl.BlockSpec((k, tn), lambda i, j: (0, col_block_of(j))),
                  tab_spec, tab_spec, tab_spec],
        out_specs=pl.BlockSpec((tm, tn), lambda i, j: (i, j)),
        out_shape=jax.ShapeDtypeStruct((m, n_blocks * tn), BF16),
        compiler_params=_params("parallel", "arbitrary"),
        name="mm_rope",
    )(a, w, c, sa, sb)


def _rope_tables(seq):
    half = ROT_DIM // 2
    inv = ROPE_THETA ** (-jnp.arange(0, ROT_DIM, 2, dtype=F32) / ROT_DIM)
    ang = jnp.arange(seq, dtype=F32)[:, None] * inv[None, :]
    cos, sin = jnp.cos(ang), jnp.sin(ang)
    ones = jnp.ones((seq, HEAD_DIM - ROT_DIM), F32)
    zeros = jnp.zeros((seq, HEAD_DIM - ROT_DIM), F32)
    zh = jnp.zeros((seq, half), F32)
    c = jnp.concatenate([cos, cos, ones], axis=1)
    sa = jnp.concatenate([zh, sin, zeros], axis=1)
    sb = jnp.concatenate([-sin, zh, zeros], axis=1)
    return c, sa, sb


def _diff_attn_body(lam_ref, q_ref, k_ref, v_ref, g_ref, o_ref, m_sc, l_sc, acc_sc,
                    *, tk, lam_init):
    tq = q_ref.shape[0]
    nk = k_ref.shape[0] // tk
    m_sc[...] = jnp.full(m_sc.shape, -jnp.inf, F32)
    l_sc[...] = jnp.zeros(l_sc.shape, F32)
    acc_sc[...] = jnp.zeros(acc_sc.shape, F32)
    q = q_ref[...]
    qs = (q[:, :HEAD_DIM], q[:, HEAD_DIM:])

    def step(c, carry):
        off = pl.multiple_of(c * tk, tk)
        ks = k_ref[pl.ds(off, tk), :]
        vs = v_ref[pl.ds(off, tk), :]
        for comp in range(2):
            s = _dot_nt(qs[comp], ks[:, comp * HEAD_DIM:(comp + 1) * HEAD_DIM])
            m_old = m_sc[comp]
            m_new = jnp.maximum(m_old, jnp.max(s, axis=1, keepdims=True))
            alpha = jnp.exp(m_old - m_new)
            p = jnp.exp(s - m_new)
            l_sc[comp] = alpha * l_sc[comp] + jnp.sum(p, axis=1, keepdims=True)
            acc_sc[comp] = alpha * acc_sc[comp] + _dot(p.astype(BF16), vs)
            m_sc[comp] = m_new
        return carry

    lax.fori_loop(0, nk, step, 0)

    lv = lam_ref[...]
    lam = (jnp.exp(jnp.sum(lv[0:1] * lv[1:2], axis=1, keepdims=True))
           - jnp.exp(jnp.sum(lv[2:3] * lv[3:4], axis=1, keepdims=True)) + lam_init)
    o = acc_sc[0] / l_sc[0] - lam * (acc_sc[1] / l_sc[1])
    ms = jnp.mean(o * o, axis=-1, keepdims=True)
    o = o * lax.rsqrt(ms + NORM_EPS) * g_ref[...]
    o_ref[...] = (o * (1.0 - lam_init)).astype(o_ref.dtype)


def _diff_attn(qk, v, lam_vecs, subln_g, lam_init, q_col0, k_col0, v_col0, tq=512, tk=512):
    s = qk.shape[0]
    hw = 2 * HEAD_DIM
    return pl.pallas_call(
        functools.partial(_diff_attn_body, tk=tk, lam_init=lam_init),
        grid=(DIFF_HEADS, s // tq),
        in_specs=[pl.BlockSpec((4, HEAD_DIM), lambda h, i: (0, 0)),
                  pl.BlockSpec((tq, hw), lambda h, i: (i, q_col0 + h)),
                  pl.BlockSpec((s, hw), lambda h, i: (0, k_col0 + h)),
                  pl.BlockSpec((s, hw), lambda h, i: (0, v_col0 + h)),
                  pl.BlockSpec((1, hw), lambda h, i: (0, 0))],
        out_specs=pl.BlockSpec((tq, hw), lambda h, i: (i, h)),
        out_shape=jax.ShapeDtypeStruct((s, DIFF_HEADS * hw), BF16),
        scratch_shapes=[pltpu.VMEM((2, tq, 1), F32),
                        pltpu.VMEM((2, tq, 1), F32),
                        pltpu.VMEM((2, tq, hw), F32)],
        compiler_params=_params("parallel", "arbitrary"),
        name="diff_attn",
    )(lam_vecs, qk, qk, v, subln_g.reshape(1, hw))


def _dil_attn_body(q_ref, kl_ref, km_ref, kr_ref, vl_ref, vm_ref, vr_ref, o_ref, lse_ref,
                   *, length):
    t = q_ref.shape[0]
    w = t + 2 * DIL_HALO
    i0 = pl.program_id(1) * t
    qi = i0 + lax.broadcasted_iota(jnp.int32, (t, w), 0)
    kj = i0 - DIL_HALO + lax.broadcasted_iota(jnp.int32, (t, w), 1)
    in_band = jnp.where(jnp.abs(qi - kj) <= DIL_HALO, 1, 0)
    in_seq = jnp.where(kj >= 0, jnp.where(kj < length, 1, 0), 0)
    valid = (in_band * in_seq) > 0
    for hh in range(DIL_HEADS_PER_GROUP):
        sl = slice(hh * HEAD_DIM, (hh + 1) * HEAD_DIM)
        kw = jnp.concatenate([kl_ref[:, sl], km_ref[:, sl], kr_ref[:, sl]], axis=0)
        vw = jnp.concatenate([vl_ref[:, sl], vm_ref[:, sl], vr_ref[:, sl]], axis=0)
        sc = _dot_nt(q_ref[:, sl], kw)
        sc = jnp.where(valid, sc, -1e30)
        m = jnp.max(sc, axis=-1, keepdims=True)
        p = jnp.exp(sc - m)
        den = jnp.sum(p, axis=-1, keepdims=True)
        o_ref[:, sl] = _dot((p / den).astype(BF16), vw)
        lse_ref[:, sl] = jnp.broadcast_to(m + jnp.log(den), (t, HEAD_DIM))


def _dil_attn(qk, v, group, dil, q_col0, k_col0, v_col0, t=256):
    s, wq = qk.shape
    wv = v.shape[1]
    length = s // dil
    t = min(t, length)
    gw = DIL_HEADS_PER_GROUP * HEAD_DIM
    bq, bv = wq // gw, wv // gw
    qk2 = qk.reshape(length, dil * wq)
    v2 = v.reshape(length, dil * wv)
    hb = t // DIL_HALO
    n_halo = length // DIL_HALO

    def main(col0, per_row):
        return pl.BlockSpec((t, gw), lambda r, lb: (lb, r * per_row + col0 + group))

    def left(col0, per_row):
        return pl.BlockSpec((DIL_HALO, gw),
                            lambda r, lb: (jnp.maximum(lb * hb - 1, 0), r * per_row + col0 + group))

    def right(col0, per_row):
        return pl.BlockSpec((DIL_HALO, gw),
                            lambda r, lb: (jnp.minimum((lb + 1) * hb, n_halo - 1),
                                           r * per_row + col0 + group))

    out_spec = pl.BlockSpec((t, gw), lambda r, lb: (lb, r))
    o, lse = pl.pallas_call(
        functools.partial(_dil_attn_body, length=length),
        grid=(dil, length // t),
        in_specs=[main(q_col0, bq),
                  left(k_col0, bq), main(k_col0, bq), right(k_col0, bq),
                  left(v_col0, bv), main(v_col0, bv), right(v_col0, bv)],
        out_specs=[out_spec, out_spec],
        out_shape=[jax.ShapeDtypeStruct((length, dil * gw), F32),
                   jax.ShapeDtypeStruct((length, dil * gw), F32)],
        compiler_params=_params("parallel", "arbitrary"),
        name=f"dil_attn_d{dil}",
    )(qk2, qk2, qk2, qk2, v2, v2, v2)
    return o.reshape(s, gw), lse.reshape(s, gw)


def _dil_combine_body(o0, o1, o2, l0, l1, l2, out_ref):
    a, b, c = l0[...], l1[...], l2[...]
    m = jnp.maximum(jnp.maximum(a, b), c)
    ea, eb, ec = jnp.exp(a - m), jnp.exp(b - m), jnp.exp(c - m)
    den = ea + eb + ec
    out = (ea / den) * o0[...] + (eb / den) * o1[...] + (ec / den) * o2[...]
    out_ref[...] = out.astype(out_ref.dtype)


def _dil_combine(outs, lses, tm=512):
    s, w = outs[0].shape
    spec = pl.BlockSpec((tm, w), lambda i: (i, 0))
    return pl.pallas_call(
        _dil_combine_body,
        grid=(s // tm,),
        in_specs=[spec] * 6,
        out_specs=spec,
        out_shape=jax.ShapeDtypeStruct((s, w), BF16),
        compiler_params=_params("parallel"),
        name="dil_combine",
    )(*outs, *lses)


def _row_mask(i, tm, seq):
    row = i * tm - HALO + lax.broadcasted_iota(jnp.int32, (tm + 2 * HALO, 1), 0)
    return jnp.where(row >= 0, jnp.where(row < seq, 1.0, 0.0), 0.0)


def _shift_rows(x, k):
    n = x.shape[0]
    return pltpu.roll(x, (-k) % n, axis=0)


def _conv_body(b_ref, cp_ref, cm_ref, cn_ref, up_ref, um_ref, un_ref, w_ref, o_ref, *, seq):
    tm = b_ref.shape[0]
    i = pl.program_id(0)
    c_ext = jnp.concatenate([cp_ref[...], cm_ref[...], cn_ref[...]], axis=0).astype(F32)
    u_ext = jnp.concatenate([up_ref[...], um_ref[...], un_ref[...]], axis=0).astype(F32)
    z = c_ext * u_ext * _row_mask(i, tm, seq)
    w = w_ref[...]
    y = w[0:1] * _shift_rows(z, -1) + w[1:2] * z + w[2:3] * _shift_rows(z, 1)
    y = y[HALO:HALO + tm]
    o_ref[...] = (b_ref[...].astype(F32) * y).astype(o_ref.dtype)


def _conv_mixer(rest, w_conv, b_col, c_col, u_col, width, tm=512):
    s = rest.shape[0]
    hb = tm // HALO
    n_halo = s // HALO

    def main(col):
        return pl.BlockSpec((tm, width), lambda i: (i, col))

    def prev(col):
        return pl.BlockSpec((HALO, width), lambda i: (jnp.maximum(i * hb - 1, 0), col))

    def nxt(col):
        return pl.BlockSpec((HALO, width), lambda i: (jnp.minimum((i + 1) * hb, n_halo - 1), col))

    return pl.pallas_call(
        functools.partial(_conv_body, seq=s),
        grid=(s // tm,),
        in_specs=[main(b_col), prev(c_col), main(c_col), nxt(c_col),
                  prev(u_col), main(u_col), nxt(u_col),
                  pl.BlockSpec(w_conv.shape, lambda i: (0, 0))],
        out_specs=pl.BlockSpec((tm, width), lambda i: (i, 0)),
        out_shape=jax.ShapeDtypeStruct((s, width), BF16),
        compiler_params=_params("parallel"),
        name="conv_mixer",
    )(rest, rest, rest, rest, rest, rest, rest, w_conv)


def _pool_body(up_ref, um_ref, un_ref, w_ref, sc_ref, o_ref, *, seq):
    tm = um_ref.shape[0]
    i = pl.program_id(0)
    ext = jnp.concatenate([up_ref[...], um_ref[...], un_ref[...]], axis=0).astype(F32)
    ext = ext * _row_mask(i, tm, seq)
    pos = i * tm + lax.broadcasted_iota(jnp.int32, (tm, 1), 0)
    gw = ext.shape[1] // len(POOL_WINDOWS)
    for gi, win in enumerate(POOL_WINDOWS):
        sl = slice(gi * gw, (gi + 1) * gw)
        xg = ext[:, sl]
        acc = xg
        span = 1
        while span < win:
            acc = acc + _shift_rows(acc, span)
            span *= 2
        total = _shift_rows(acc, -(win // 2))[HALO:HALO + tm]
        lo = jnp.clip(pos - win // 2, 0, seq)
        hi = jnp.clip(pos - win // 2 + win, 0, seq)
        cnt = (hi - lo).astype(F32)
        pooled = total / cnt - xg[HALO:HALO + tm]
        y = _dot(pooled.astype(BF16), w_ref[gi])
        o_ref[:, sl] = (y * sc_ref[:, sl]).astype(o_ref.dtype)


def _pool_mixer(rest, w_pool, pool_scale, u_col, width, tm=512):
    s = rest.shape[0]
    hb = tm // HALO
    n_halo = s // HALO
    return pl.pallas_call(
        functools.partial(_pool_body, seq=s),
        grid=(s // tm,),
        in_specs=[pl.BlockSpec((HALO, width), lambda i: (jnp.maximum(i * hb - 1, 0), u_col)),
                  pl.BlockSpec((tm, width), lambda i: (i, u_col)),
                  pl.BlockSpec((HALO, width), lambda i: (jnp.minimum((i + 1) * hb, n_halo - 1), u_col)),
                  pl.BlockSpec(w_pool.shape, lambda i: (0, 0, 0)),
                  pl.BlockSpec((1, width), lambda i: (0, 0))],
        out_specs=pl.BlockSpec((tm, width), lambda i: (i, 0)),
        out_shape=jax.ShapeDtypeStruct((s, width), BF16),
        compiler_params=_params("parallel"),
        name="pool_mixer",
    )(rest, rest, rest, w_pool, pool_scale.reshape(1, width))


def _merge_body(ya_ref, yb_ref, yc_ref, yd_ref, gl_ref, wa_ref, wb_ref, wc_ref, wd_ref,
                g0_ref, g1_ref, g2_ref, g3_ref, b0_ref, b1_ref, b2_ref, b3_ref, o_ref):
    gl = gl_ref[...]
    branches = ((ya_ref, wa_ref, g0_ref, b0_ref), (yb_ref, wb_ref, g1_ref, b1_ref),
                (yc_ref, wc_ref, g2_ref, b2_ref), (yd_ref, wd_ref, g3_ref, b3_ref))
    merged = None
    for y_ref, w_ref, g_ref, b_ref in branches:
        gate = jax.nn.sigmoid(_dot(gl, g_ref[...]) + b_ref[...])
        term = gate * _dot(y_ref[...], w_ref[...])
        merged = term if merged is None else merged + term
    o_ref[...] = merged.astype(o_ref.dtype)


def _merge(ya, yb, yc, yd, rest, gl_col, w_a, w_b, w_c, w_d, w_gate_up, b_gate, tm=512, tn=512):
    s = ya.shape[0]
    d = w_a.shape[1]
    rank = w_gate_up.shape[0]
    nj = d // tn

    def act(arr):
        return pl.BlockSpec((tm, arr.shape[1]), lambda i, j: (i, 0))

    def wgt(arr):
        return pl.BlockSpec((arr.shape[0], tn), lambda i, j: (0, j))

    def gate_w(b):
        return pl.BlockSpec((rank, tn), lambda i, j: (0, b * nj + j))

    def gate_b(b):
        return pl.BlockSpec((1, tn), lambda i, j: (0, b * nj + j))

    bias = b_gate.reshape(1, N_BRANCHES * d)
    return pl.pallas_call(
        _merge_body,
        grid=(s // tm, nj),
        in_specs=[act(ya), act(yb), act(yc), act(yd),
                  pl.BlockSpec((tm, rank), lambda i, j: (i, gl_col)),
                  wgt(w_a), wgt(w_b), wgt(w_c), wgt(w_d),
                  gate_w(0), gate_w(1), gate_w(2), gate_w(3),
                  gate_b(0), gate_b(1), gate_b(2), gate_b(3)],
        out_specs=pl.BlockSpec((tm, tn), lambda i, j: (i, j)),
        out_shape=jax.ShapeDtypeStruct((s, d), BF16),
        compiler_params=_params("parallel", "arbitrary"),
        name="merge",
    )(ya, yb, yc, yd, rest, w_a, w_b, w_c, w_d,
      w_gate_up, w_gate_up, w_gate_up, w_gate_up, bias, bias, bias, bias)


def kernel(x, ln_ffn1, w1_gate, w1_up, w1_down, ln_mix, w_in, lambda_q1, lambda_k1, lambda_q2, lambda_k2, subln, conv_w, pool_w, pool_scale, w_gate_up, b_gate, w_branch_a, w_branch_b, w_branch_c, w_branch_d, w_out, ln_ffn2, w2_gate, w2_up, w2_down, ln_final):
    b, s, d = x.shape
    depth = ln_ffn1.shape[0]
    mix = 3 * d // 8
    tn = 512
    mb = mix // tn
    tables = _rope_tables(s)
    bf = lambda w: w.astype(BF16)

    outs = []
    for bi in range(b):
        xs = x[bi]
        for l in range(depth):
            h = _rmsnorm(xs, ln_ffn1[l], BF16)
            a = _ffn_up(h, bf(w1_gate[l]), bf(w1_up[l]))
            xs = _mm_residual(a, bf(w1_down[l]), xs, 0.5)

            h = _rmsnorm(xs, ln_mix[l], BF16)
            w_in_l = bf(w_in[l])
            qk = _mm_rope(h, w_in_l, tables, lambda j: jnp.where(j < 2 * mb, j, j + mb),
                          4 * mb, mb)
            n_rest = w_in.shape[2] // tn - 4 * mb
            rest = _mm_cols(h, w_in_l, lambda j: jnp.where(j < mb, j + 2 * mb, j + 4 * mb), n_rest)

            lam_init = 0.8 - 0.6 * math.exp(-0.3 * l)
            lam_vecs = jnp.stack([lambda_q1[l], lambda_k1[l], lambda_q2[l], lambda_k2[l]])
            ya = _diff_attn(qk, rest, lam_vecs, subln[l], lam_init,
                            q_col0=0, k_col0=DIFF_HEADS, v_col0=0)

            dil_o, dil_lse = [], []
            for g, (window, dil) in enumerate(DIL_PATTERNS):
                assert window // (2 * dil) == DIL_HALO
                o_g, lse_g = _dil_attn(qk, rest, g, dil, q_col0=2 * mb, k_col0=3 * mb, v_col0=mb)
                dil_o.append(o_g)
                dil_lse.append(lse_g)
            yb = _dil_combine(dil_o, dil_lse)

            yc = _conv_mixer(rest, conv_w[l], b_col=2, c_col=3, u_col=4, width=mix)
            yd = _pool_mixer(rest, bf(pool_w[l]), pool_scale[l], u_col=5, width=mix)

            merged = _merge(ya, yb, yc, yd, rest, 6 * mb, bf(w_branch_a[l]), bf(w_branch_b[l]),
                            bf(w_branch_c[l]), bf(w_branch_d[l]), bf(w_gate_up[l]), b_gate[l])
            xs = _mm_residual(merged, bf(w_out[l]), xs, 1.0)

            h = _rmsnorm(xs, ln_ffn2[l], BF16)
            a = _ffn_up(h, bf(w2_gate[l]), bf(w2_up[l]))
            xs = _mm_residual(a, bf(w2_down[l]), xs, 0.5)
        outs.append(_rmsnorm(xs, ln_final, F32))
    return jnp.stack(outs, axis=0)
```

```python
import functools
import math

import jax
import jax.numpy as jnp
from jax import lax
from jax.experimental import pallas as pl
from jax.experimental.pallas import tpu as pltpu

F32 = jnp.float32
BF16 = jnp.bfloat16

LANES = 128
HEAD_DIM = 128
ROT_DIM = HEAD_DIM // 4
ROPE_THETA = 500000.0
NORM_EPS = 1e-5
DIFF_HEADS = 6
DIL_PATTERNS = ((128, 1), (512, 4), (2048, 16))
DIL_HEADS_PER_GROUP = 4
DIL_RADIUS = 64
POOL_WINDOWS = (2, 4, 8, 16)
N_BRANCHES = 4
HALO = 16
MASK_VALUE = -1e30

VMEM_LIMIT_BYTES = 56 * 1024 * 1024


def _params(*sem):
    return pltpu.CompilerParams(dimension_semantics=sem, vmem_limit_bytes=VMEM_LIMIT_BYTES)


def _dot(a, b):
    return jnp.dot(a, b, preferred_element_type=F32)


def _dot_nt(a, b):
    return lax.dot_general(a, b, (((1,), (1,)), ((), ())), preferred_element_type=F32)


def _layer_cols(layer, k, tn, col_of):
    return pl.BlockSpec((None, k, tn), lambda i, j: (layer, 0, col_of(j)))


def _rmsnorm_body(x_ref, g_ref, o_ref):
    x = x_ref[...]
    ms = jnp.mean(x * x, axis=-1, keepdims=True)
    o_ref[...] = (x * lax.rsqrt(ms + NORM_EPS) * g_ref[...]).astype(o_ref.dtype)


def _rmsnorm(x, g, out_dtype, tm=256):
    s, d = x.shape
    return pl.pallas_call(
        _rmsnorm_body,
        grid=(s // tm,),
        in_specs=[pl.BlockSpec((tm, d), lambda i: (i, 0)),
                  pl.BlockSpec((1, d), lambda i: (0, 0))],
        out_specs=pl.BlockSpec((tm, d), lambda i: (i, 0)),
        out_shape=jax.ShapeDtypeStruct((s, d), out_dtype),
        compiler_params=_params("parallel"),
        name="rmsnorm",
    )(x, g.reshape(1, d))


def _ffn_up_body(h_ref, wg_ref, wu_ref, o_ref):
    h = h_ref[...]
    g = _dot(h, wg_ref[...].astype(BF16))
    u = _dot(h, wu_ref[...].astype(BF16))
    o_ref[...] = (g * jax.nn.sigmoid(g) * u).astype(o_ref.dtype)


def _ffn_up(h, wg, wu, layer, tm=1024, tn=256):
    m, k = h.shape
    n = wg.shape[2]
    return pl.pallas_call(
        _ffn_up_body,
        grid=(m // tm, n // tn),
        in_specs=[pl.BlockSpec((tm, k), lambda i, j: (i, 0)),
                  _layer_cols(layer, k, tn, lambda j: j),
                  _layer_cols(layer, k, tn, lambda j: j)],
        out_specs=pl.BlockSpec((tm, tn), lambda i, j: (i, j)),
        out_shape=jax.ShapeDtypeStruct((m, n), BF16),
        compiler_params=_params("parallel", "arbitrary"),
        name="ffn_up",
    )(h, wg, wu)


def _mm_residual_body(a_ref, w_ref, r_ref, o_ref, *, alpha):
    o_ref[...] = r_ref[...] + alpha * _dot(a_ref[...], w_ref[...].astype(BF16))


def _mm_residual(a, w, layer, res, alpha, tm=1024, tn=512):
    m, k = a.shape
    n = w.shape[2]
    return pl.pallas_call(
        functools.partial(_mm_residual_body, alpha=alpha),
        grid=(m // tm, n // tn),
        in_specs=[pl.BlockSpec((tm, k), lambda i, j: (i, 0)),
                  _layer_cols(layer, k, tn, lambda j: j),
                  pl.BlockSpec((tm, tn), lambda i, j: (i, j))],
        out_specs=pl.BlockSpec((tm, tn), lambda i, j: (i, j)),
        out_shape=jax.ShapeDtypeStruct((m, n), F32),
        compiler_params=_params("parallel", "arbitrary"),
        name="mm_residual",
    )(a, w, res)


def _mm_cols_body(a_ref, w_ref, o_ref):
    o_ref[...] = _dot(a_ref[...], w_ref[...].astype(BF16)).astype(o_ref.dtype)


def _mm_cols(a, w, layer, col_block_of, n_blocks, tm=1024, tn=512):
    m, k = a.shape
    return pl.pallas_call(
        _mm_cols_body,
        grid=(m // tm, n_blocks),
        in_specs=[pl.BlockSpec((tm, k), lambda i, j: (i, 0)),
                  _layer_cols(layer, k, tn, col_block_of)],
        out_specs=pl.BlockSpec((tm, tn), lambda i, j: (i, j)),
        out_shape=jax.ShapeDtypeStruct((m, n_blocks * tn), BF16),
        compiler_params=_params("parallel", "arbitrary"),
        name="mm_cols",
    )(a, w)


def _mm_rope_body(a_ref, w_ref, c_ref, sa_ref, sb_ref, o_ref, *, q_scale, blocks_per_part):
    acc = _dot(a_ref[...], w_ref[...].astype(BF16))
    tn = acc.shape[1]
    rep = tn // HEAD_DIM
    c = jnp.concatenate([c_ref[...]] * rep, axis=1)
    sa = jnp.concatenate([sa_ref[...]] * rep, axis=1)
    sb = jnp.concatenate([sb_ref[...]] * rep, axis=1)
    half = ROT_DIM // 2
    lo = pltpu.roll(acc, half, axis=1)
    hi = pltpu.roll(acc, tn - half, axis=1)
    out = acc * c + lo * sa + hi * sb
    j = pl.program_id(1)
    is_q = (j // blocks_per_part) % 2 == 0
    out = out * jnp.where(is_q, q_scale, 1.0)
    o_ref[...] = out.astype(o_ref.dtype)


def _mm_rope(a, w, layer, tables, col_block_of, n_blocks, blocks_per_part, tm=1024, tn=256):
    m, k = a.shape
    c, sa, sb = tables
    tab_spec = pl.BlockSpec((tm, HEAD_DIM), lambda i, j: (i, 0))
    return pl.pallas_call(
        functools.partial(_mm_rope_body, q_scale=HEAD_DIM ** -0.5, blocks_per_part=blocks_per_part),
        grid=(m // tm, n_blocks),
        in_specs=[pl.BlockSpec((tm, k), lambda i, j: (i, 0)),
                  _layer_cols(layer, k, tn, col_block_of),
                  tab_spec, tab_spec, tab_spec],
        out_specs=pl.BlockSpec((tm, tn), lambda i, j: (i, j)),
        out_shape=jax.ShapeDtypeStruct((m, n_blocks * tn), BF16),
        compiler_params=_params("parallel", "arbitrary"),
        name="mm_rope",
    )(a, w, c, sa, sb)


def _rope_tables(seq):
    half = ROT_DIM // 2
    inv = ROPE_THETA ** (-jnp.arange(0, ROT_DIM, 2, dtype=F32) / ROT_DIM)
    ang = jnp.arange(seq, dtype=F32)[:, None] * inv[None, :]
    cos, sin = jnp.cos(ang), jnp.sin(ang)
    ones = jnp.ones((seq, HEAD_DIM - ROT_DIM), F32)
    zeros = jnp.zeros((seq, HEAD_DIM - ROT_DIM), F32)
    zh = jnp.zeros((seq, half), F32)
    c = jnp.concatenate([cos, cos, ones], axis=1)
    sa = jnp.concatenate([zh, sin, zeros], axis=1)
    sb = jnp.concatenate([-sin, zh, zeros], axis=1)
    return c, sa, sb


def _diff_attn_body(lam_ref, q_ref, k_ref, v_ref, g_ref, o_ref,
                    s_scr, mpart_scr, m_scr, lpart_scr, acc_scr, *, tu, tk, lam_init):
    tq = q_ref.shape[0]
    nk = k_ref.shape[0] // tk
    nb = tk // LANES
    units = [(r, c) for r in range(tq // tu) for c in range(2)]

    def scores_chunk(u, ch):
        r, c = units[u]
        q = q_ref[r * tu:(r + 1) * tu, c * HEAD_DIM:(c + 1) * HEAD_DIM]
        k = k_ref[pl.ds(pl.multiple_of(ch * tk, tk), tk), c * HEAD_DIM:(c + 1) * HEAD_DIM]
        s = _dot_nt(q, k)
        s_scr[u % 2, ch] = s
        mp = s[:, :LANES]
        for b in range(1, nb):
            mp = jnp.maximum(mp, s[:, b * LANES:(b + 1) * LANES])
        mpart_scr[u % 2] = jnp.maximum(mpart_scr[u % 2], mp)

    def scores_begin(u):
        mpart_scr[u % 2] = jnp.full((tu, LANES), -jnp.inf, F32)

    def scores_end(u):
        m = jnp.max(mpart_scr[u % 2], axis=1, keepdims=True)
        m_scr[u % 2] = jnp.broadcast_to(m, (tu, LANES))

    def values_chunk(u, ch):
        c = units[u][1]
        m = m_scr[u % 2]
        p = jnp.exp(s_scr[u % 2, ch] - jnp.concatenate([m] * nb, axis=1))
        lp = p[:, :LANES]
        for b in range(1, nb):
            lp = lp + p[:, b * LANES:(b + 1) * LANES]
        lpart_scr[...] += lp
        v = v_ref[pl.ds(pl.multiple_of(ch * tk, tk), tk), :]
        acc_scr[c] += _dot(p.astype(BF16), v)

    def values_begin(u):
        lpart_scr[...] = jnp.zeros((tu, LANES), F32)
        acc_scr[units[u][1]] = jnp.zeros((tu, 2 * HEAD_DIM), F32)

    def values_end(u):
        r, c = units[u]
        l = jnp.sum(lpart_scr[...], axis=1, keepdims=True)
        acc_scr[c] = acc_scr[c] / l
        if c == 1:
            lv = lam_ref[...]
            lam = (jnp.exp(jnp.sum(lv[0:1] * lv[1:2], axis=1, keepdims=True))
                   - jnp.exp(jnp.sum(lv[2:3] * lv[3:4], axis=1, keepdims=True)) + lam_init)
            o = acc_scr[0] - lam * acc_scr[1]
            ms = jnp.mean(o * o, axis=-1, keepdims=True)
            o = o * lax.rsqrt(ms + NORM_EPS) * g_ref[...]
            o_ref[r * tu:(r + 1) * tu, :] = (o * (1.0 - lam_init)).astype(o_ref.dtype)

    def loop(fn):
        def body(ch, carry):
            fn(ch)
            return carry
        lax.fori_loop(0, nk, body, 0, unroll=min(nk, 4))

    last = len(units) - 1
    scores_begin(0)
    loop(lambda ch: scores_chunk(0, ch))
    scores_end(0)
    for u in range(last):
        scores_begin(u + 1)
        values_begin(u)

        def both(ch, u=u):
            scores_chunk(u + 1, ch)
            values_chunk(u, ch)

        loop(both)
        scores_end(u + 1)
        values_end(u)
    values_begin(last)
    loop(lambda ch: values_chunk(last, ch))
    values_end(last)


def _diff_attn(qk, v, lam_vecs, subln_g, lam_init, q_col0, k_col0, v_col0, tq=1024, tu=256, tk=1024):
    s = qk.shape[0]
    hw = 2 * HEAD_DIM
    return pl.pallas_call(
        functools.partial(_diff_attn_body, tu=tu, tk=tk, lam_init=lam_init),
        grid=(DIFF_HEADS, s // tq),
        in_specs=[pl.BlockSpec((4, HEAD_DIM), lambda h, i: (0, 0)),
                  pl.BlockSpec((tq, hw), lambda h, i: (i, q_col0 + h)),
                  pl.BlockSpec((s, hw), lambda h, i: (0, k_col0 + h)),
                  pl.BlockSpec((s, hw), lambda h, i: (0, v_col0 + h)),
                  pl.BlockSpec((1, hw), lambda h, i: (0, 0))],
        out_specs=pl.BlockSpec((tq, hw), lambda h, i: (i, h)),
        out_shape=jax.ShapeDtypeStruct((s, DIFF_HEADS * hw), BF16),
        scratch_shapes=[pltpu.VMEM((2, s // tk, tu, tk), F32),
                        pltpu.VMEM((2, tu, LANES), F32),
                        pltpu.VMEM((2, tu, LANES), F32),
                        pltpu.VMEM((tu, LANES), F32),
                        pltpu.VMEM((2, tu, hw), F32)],
        compiler_params=_params("parallel", "arbitrary"),
        name="diff_attn",
    )(lam_vecs, qk, qk, v, subln_g.reshape(1, hw))


def _dil_window(i, t, dil, seq):
    reach = DIL_RADIUS * dil
    width = t + 2 * reach
    start = jnp.clip(i * t - reach, 0, seq - width)
    return start, width


def _dil_attn_body(*refs, t, seq):
    n = len(DIL_PATTERNS)
    q_refs, k_refs, v_refs = refs[:n], refs[n:2 * n], refs[2 * n:3 * n]
    o_ref = refs[3 * n]
    bias_refs = refs[3 * n + 1:]
    i = pl.program_id(1)
    outs, lses = [], []
    for g, (_, dil) in enumerate(DIL_PATTERNS):
        start, width = _dil_window(i, t, dil, seq)
        start = pl.multiple_of(start, DIL_RADIUS)
        offset = start - i * t
        prev_start, _ = _dil_window(i - 1, t, dil, seq)
        bias_ref = bias_refs[g]

        @pl.when((i == 0) | (offset != prev_start - (i - 1) * t))
        def _(bias_ref=bias_ref, offset=offset, width=width, dil=dil):
            rel = (offset + lax.broadcasted_iota(jnp.int32, (t, width), 1)
                   - lax.broadcasted_iota(jnp.int32, (t, width), 0))
            off_band = jnp.where(jnp.abs(rel) <= DIL_RADIUS * dil, rel & (dil - 1), 1)
            bias_ref[...] = jnp.where(off_band == 0, 0.0, MASK_VALUE)

        kw = k_refs[g][pl.ds(start, width), :]
        vw = v_refs[g][pl.ds(start, width), :]
        sc = _dot_nt(q_refs[g][...], kw) + bias_ref[...]
        m = jnp.max(sc, axis=-1, keepdims=True)
        p = jnp.exp(sc - m)
        den = jnp.sum(p, axis=-1, keepdims=True)
        outs.append(_dot(p.astype(BF16), vw) / den)
        lses.append(m + jnp.log(den))
    top = functools.reduce(jnp.maximum, lses)
    ws = [jnp.exp(lse - top) for lse in lses]
    total = functools.reduce(lambda a, b: a + b, ws)
    out = functools.reduce(lambda a, b: a + b, [(w / total) * o for w, o in zip(ws, outs)])
    o_ref[...] = out.astype(o_ref.dtype)


def _dil_attn(qk, v, q_col0, k_col0, v_col0, t=256):
    s = qk.shape[0]
    hg = DIL_HEADS_PER_GROUP
    n = len(DIL_PATTERNS)

    def block(col0, g):
        return pl.BlockSpec((t, HEAD_DIM), lambda hh, i: (i, col0 + hg * g + hh))

    def full(col0, g):
        return pl.BlockSpec((s, HEAD_DIM), lambda hh, i: (0, col0 + hg * g + hh))

    widths = [t + 2 * DIL_RADIUS * dil for _, dil in DIL_PATTERNS]
    assert all(window // (2 * dil) == DIL_RADIUS for window, dil in DIL_PATTERNS)
    return pl.pallas_call(
        functools.partial(_dil_attn_body, t=t, seq=s),
        grid=(hg, s // t),
        in_specs=([block(q_col0, g) for g in range(n)] + [full(k_col0, g) for g in range(n)]
                  + [full(v_col0, g) for g in range(n)]),
        out_specs=pl.BlockSpec((t, HEAD_DIM), lambda hh, i: (i, hh)),
        out_shape=jax.ShapeDtypeStruct((s, hg * HEAD_DIM), BF16),
        scratch_shapes=[pltpu.VMEM((t, w), F32) for w in widths],
        compiler_params=_params("arbitrary", "arbitrary"),
        name="dil_attn",
    )(*([qk] * n), *([qk] * n), *([v] * n))


def _row_mask(i, tm, seq):
    row = i * tm - HALO + lax.broadcasted_iota(jnp.int32, (tm + 2 * HALO, 1), 0)
    return jnp.where(row >= 0, jnp.where(row < seq, 1.0, 0.0), 0.0)


def _shift_rows(x, k):
    n = x.shape[0]
    return pltpu.roll(x, (-k) % n, axis=0)


def _conv_body(b_ref, cp_ref, cm_ref, cn_ref, up_ref, um_ref, un_ref, w_ref, o_ref, *, seq):
    tm = b_ref.shape[0]
    i = pl.program_id(0)
    c_ext = jnp.concatenate([cp_ref[...], cm_ref[...], cn_ref[...]], axis=0).astype(F32)
    u_ext = jnp.concatenate([up_ref[...], um_ref[...], un_ref[...]], axis=0).astype(F32)
    z = c_ext * u_ext * _row_mask(i, tm, seq)
    w = w_ref[...]
    y = w[0:1] * _shift_rows(z, -1) + w[1:2] * z + w[2:3] * _shift_rows(z, 1)
    y = y[HALO:HALO + tm]
    o_ref[...] = (b_ref[...].astype(F32) * y).astype(o_ref.dtype)


def _conv_mixer(rest, w_conv, layer, b_col, c_col, u_col, width, tm=512):
    s = rest.shape[0]
    hb = tm // HALO
    n_halo = s // HALO

    def main(col):
        return pl.BlockSpec((tm, width), lambda i: (i, col))

    def prev(col):
        return pl.BlockSpec((HALO, width), lambda i: (jnp.maximum(i * hb - 1, 0), col))

    def nxt(col):
        return pl.BlockSpec((HALO, width), lambda i: (jnp.minimum((i + 1) * hb, n_halo - 1), col))

    return pl.pallas_call(
        functools.partial(_conv_body, seq=s),
        grid=(s // tm,),
        in_specs=[main(b_col), prev(c_col), main(c_col), nxt(c_col),
                  prev(u_col), main(u_col), nxt(u_col),
                  pl.BlockSpec((None,) + w_conv.shape[1:], lambda i: (layer, 0, 0))],
        out_specs=pl.BlockSpec((tm, width), lambda i: (i, 0)),
        out_shape=jax.ShapeDtypeStruct((s, width), BF16),
        compiler_params=_params("parallel"),
        name="conv_mixer",
    )(rest, rest, rest, rest, rest, rest, rest, w_conv)


def _pool_body(up_ref, um_ref, un_ref, w_ref, sc_ref, o_ref, *, seq):
    tm = um_ref.shape[0]
    i = pl.program_id(0)
    ext = jnp.concatenate([up_ref[...], um_ref[...], un_ref[...]], axis=0).astype(F32)
    ext = ext * _row_mask(i, tm, seq)
    pos = i * tm + lax.broadcasted_iota(jnp.int32, (tm, 1), 0)
    gw = ext.shape[1] // len(POOL_WINDOWS)
    for gi, win in enumerate(POOL_WINDOWS):
        sl = slice(gi * gw, (gi + 1) * gw)
        xg = ext[:, sl]
        acc = xg
        span = 1
        while span < win:
            acc = acc + _shift_rows(acc, span)
            span *= 2
        total = _shift_rows(acc, -(win // 2))[HALO:HALO + tm]
        lo = jnp.clip(pos - win // 2, 0, seq)
        hi = jnp.clip(pos - win // 2 + win, 0, seq)
        cnt = (hi - lo).astype(F32)
        pooled = total / cnt - xg[HALO:HALO + tm]
        y = _dot(pooled.astype(BF16), w_ref[gi].astype(BF16))
        o_ref[:, sl] = (y * sc_ref[:, sl]).astype(o_ref.dtype)


def _pool_mixer(rest, w_pool, pool_scale, layer, u_col, width, tm=512):
    s = rest.shape[0]
    hb = tm // HALO
    n_halo = s // HALO
    return pl.pallas_call(
        functools.partial(_pool_body, seq=s),
        grid=(s // tm,),
        in_specs=[pl.BlockSpec((HALO, width), lambda i: (jnp.maximum(i * hb - 1, 0), u_col)),
                  pl.BlockSpec((tm, width), lambda i: (i, u_col)),
                  pl.BlockSpec((HALO, width), lambda i: (jnp.minimum((i + 1) * hb, n_halo - 1), u_col)),
                  pl.BlockSpec((None,) + w_pool.shape[1:], lambda i: (layer, 0, 0, 0)),
                  pl.BlockSpec((1, width), lambda i: (0, 0))],
        out_specs=pl.BlockSpec((tm, width), lambda i: (i, 0)),
        out_shape=jax.ShapeDtypeStruct((s, width), BF16),
        compiler_params=_params("parallel"),
        name="pool_mixer",
    )(rest, rest, rest, w_pool, pool_scale.reshape(1, width))


def _merge_body(ya_ref, yb_ref, yc_ref, yd_ref, gl_ref, wa_ref, wb_ref, wc_ref, wd_ref,
                g0_ref, g1_ref, g2_ref, g3_ref, b0_ref, b1_ref, b2_ref, b3_ref, o_ref):
    gl = gl_ref[...]
    branches = ((ya_ref, wa_ref, g0_ref, b0_ref), (yb_ref, wb_ref, g1_ref, b1_ref),
                (yc_ref, wc_ref, g2_ref, b2_ref), (yd_ref, wd_ref, g3_ref, b3_ref))
    merged = None
    for y_ref, w_ref, g_ref, b_ref in branches:
        gate = jax.nn.sigmoid(_dot(gl, g_ref[...].astype(BF16)) + b_ref[...])
        term = gate * _dot(y_ref[...], w_ref[...].astype(BF16))
        merged = term if merged is None else merged + term
    o_ref[...] = merged.astype(o_ref.dtype)


def _merge(ya, yb, yc, yd, rest, gl_col, w_a, w_b, w_c, w_d, w_gate_up, b_gate, layer,
           tm=1024, tn=256):
    s = ya.shape[0]
    d = w_a.shape[2]
    rank = w_gate_up.shape[1]
    nj = d // tn

    def act(arr):
        return pl.BlockSpec((tm, arr.shape[1]), lambda i, j: (i, 0))

    def wgt(arr):
        return _layer_cols(layer, arr.shape[1], tn, lambda j: j)

    def gate_w(b):
        return _layer_cols(layer, rank, tn, lambda j: b * nj + j)

    def gate_b(b):
        return _layer_cols(layer, 1, tn, lambda j: b * nj + j)

    bias = b_gate.reshape(b_gate.shape[0], 1, N_BRANCHES * d)
    return pl.pallas_call(
        _merge_body,
        grid=(s // tm, nj),
        in_specs=[act(ya), act(yb), act(yc), act(yd),
                  pl.BlockSpec((tm, rank), lambda i, j: (i, gl_col)),
                  wgt(w_a), wgt(w_b), wgt(w_c), wgt(w_d),
                  gate_w(0), gate_w(1), gate_w(2), gate_w(3),
                  gate_b(0), gate_b(1), gate_b(2), gate_b(3)],
        out_specs=pl.BlockSpec((tm, tn), lambda i, j: (i, j)),
        out_shape=jax.ShapeDtypeStruct((s, d), BF16),
        compiler_params=_params("parallel", "arbitrary"),
        name="merge",
    )(ya, yb, yc, yd, rest, w_a, w_b, w_c, w_d,
      w_gate_up, w_gate_up, w_gate_up, w_gate_up, bias, bias, bias, bias)


def kernel(x, ln_ffn1, w1_gate, w1_up, w1_down, ln_mix, w_in, lambda_q1, lambda_k1, lambda_q2, lambda_k2, subln, conv_w, pool_w, pool_scale, w_gate_up, b_gate, w_branch_a, w_branch_b, w_branch_c, w_branch_d, w_out, ln_ffn2, w2_gate, w2_up, w2_down, ln_final):
    b, s, d = x.shape
    depth = ln_ffn1.shape[0]
    mix = 3 * d // 8
    tables = _rope_tables(s)

    tn_rope, tn_cols = 256, 512
    pr, pc = mix // tn_rope, mix // tn_cols
    n_rest = (w_in.shape[2] - 4 * mix) // tn_cols
    hd_per_mix = mix // HEAD_DIM

    outs = []
    for bi in range(b):
        xs = x[bi]
        for l in range(depth):
            h = _rmsnorm(xs, ln_ffn1[l], BF16)
            a = _ffn_up(h, w1_gate, w1_up, l)
            xs = _mm_residual(a, w1_down, l, xs, 0.5)

            h = _rmsnorm(xs, ln_mix[l], BF16)
            qk = _mm_rope(h, w_in, l, tables, lambda j: jnp.where(j < 2 * pr, j, j + pr),
                          4 * pr, pr, tn=tn_rope)
            rest = _mm_cols(h, w_in, l, lambda j: jnp.where(j < pc, j + 2 * pc, j + 4 * pc),
                            n_rest, tn=tn_cols)

            lam_init = 0.8 - 0.6 * math.exp(-0.3 * l)
            lam_vecs = jnp.stack([lambda_q1[l], lambda_k1[l], lambda_q2[l], lambda_k2[l]])
            ya = _diff_attn(qk, rest, lam_vecs, subln[l], lam_init,
                            q_col0=0, k_col0=DIFF_HEADS, v_col0=0)

            yb = _dil_attn(qk, rest, q_col0=2 * hd_per_mix, k_col0=3 * hd_per_mix,
                           v_col0=hd_per_mix)

            yc = _conv_mixer(rest, conv_w, l, b_col=2, c_col=3, u_col=4, width=mix)
            yd = _pool_mixer(rest, pool_w, pool_scale[l], l, u_col=5, width=mix)

            rank = w_gate_up.shape[1]
            merged = _merge(ya, yb, yc, yd, rest, 6 * mix // rank, w_branch_a, w_branch_b,
                            w_branch_c, w_branch_d, w_gate_up, b_gate, l)
            xs = _mm_residual(merged, w_out, l, xs, 1.0)

            h = _rmsnorm(xs, ln_ffn2[l], BF16)
            a = _ffn_up(h, w2_gate, w2_up, l)
            xs = _mm_residual(a, w2_down, l, xs, 0.5)
        outs.append(_rmsnorm(xs, ln_final, F32))
    return jnp.stack(outs, axis=0)
```

```python
import functools
import math

import jax
import jax.numpy as jnp
from jax import lax
from jax.experimental import pallas as pl
from jax.experimental.pallas import tpu as pltpu

F32 = jnp.float32
BF16 = jnp.bfloat16

LANES = 128
HEAD_DIM = 128
ROT_DIM = HEAD_DIM // 4
ROPE_THETA = 500000.0
NORM_EPS = 1e-5
DIFF_HEADS = 6
DIL_PATTERNS = ((128, 1), (512, 4), (2048, 16))
DIL_HEADS_PER_GROUP = 4
DIL_RADIUS = 64
POOL_WINDOWS = (2, 4, 8, 16)
N_BRANCHES = 4
HALO = 16
MASK_VALUE = -1e30

VMEM_LIMIT_BYTES = 56 * 1024 * 1024


def _params(*sem):
    return pltpu.CompilerParams(dimension_semantics=sem, vmem_limit_bytes=VMEM_LIMIT_BYTES)


def _dot(a, b):
    return jnp.dot(a, b, preferred_element_type=F32)


def _dot_nt(a, b):
    return lax.dot_general(a, b, (((1,), (1,)), ((), ())), preferred_element_type=F32)


def _layer_cols(layer, k, tn, col_of):
    return pl.BlockSpec((None, k, tn), lambda i, j: (layer, 0, col_of(j)))


def _rmsnorm_body(x_ref, g_ref, o_ref):
    x = x_ref[...]
    ms = jnp.mean(x * x, axis=-1, keepdims=True)
    o_ref[...] = (x * lax.rsqrt(ms + NORM_EPS) * g_ref[...]).astype(o_ref.dtype)


def _rmsnorm(x, g, out_dtype, tm=256):
    s, d = x.shape
    return pl.pallas_call(
        _rmsnorm_body,
        grid=(s // tm,),
        in_specs=[pl.BlockSpec((tm, d), lambda i: (i, 0)),
                  pl.BlockSpec((1, d), lambda i: (0, 0))],
        out_specs=pl.BlockSpec((tm, d), lambda i: (i, 0)),
        out_shape=jax.ShapeDtypeStruct((s, d), out_dtype),
        compiler_params=_params("parallel"),
        name="rmsnorm",
    )(x, g.reshape(1, d))


def _lane_partial_sumsq(y):
    acc = None
    for blk in range(y.shape[1] // LANES):
        part = y[:, blk * LANES:(blk + 1) * LANES]
        acc = part * part if acc is None else acc + part * part
    return acc


def _row_rsqrt(ss_ref, width):
    return lax.rsqrt(jnp.sum(ss_ref[...], axis=1, keepdims=True) / width + NORM_EPS)


def _gained_bf16(w_ref, g_ref):
    w = w_ref[...]
    return (w * jnp.concatenate([g_ref[...]] * (w.shape[1] // LANES), axis=1)).astype(BF16)


def _lane_replicated(g):
    return jnp.broadcast_to(g[:, None], (g.shape[0], LANES))


def _prep_body(x_ref, xb_ref, ss_ref):
    x = x_ref[...]
    xb_ref[...] = x.astype(BF16)
    ss_ref[...] = _lane_partial_sumsq(x)


def _prep(x, tm=256):
    s, d = x.shape
    return pl.pallas_call(
        _prep_body,
        grid=(s // tm,),
        in_specs=[pl.BlockSpec((tm, d), lambda i: (i, 0))],
        out_specs=[pl.BlockSpec((tm, d), lambda i: (i, 0)),
                   pl.BlockSpec((tm, LANES), lambda i: (i, 0))],
        out_shape=[jax.ShapeDtypeStruct((s, d), BF16),
                   jax.ShapeDtypeStruct((s, LANES), F32)],
        compiler_params=_params("parallel"),
        name="prep",
    )(x)


def _normed_specs(tm, k):
    return [pl.BlockSpec((tm, k), lambda i, j: (i, 0)),
            pl.BlockSpec((tm, LANES), lambda i, j: (i, 0)),
            pl.BlockSpec((k, LANES), lambda i, j: (0, 0))]


def _ffn_up_body(x_ref, ss_ref, g_ref, wg_ref, wu_ref, o_ref):
    x = x_ref[...]
    r = _row_rsqrt(ss_ref, x.shape[1])
    g = r * _dot(x, _gained_bf16(wg_ref, g_ref))
    u = r * _dot(x, _gained_bf16(wu_ref, g_ref))
    o_ref[...] = (g * jax.nn.sigmoid(g) * u).astype(o_ref.dtype)


def _ffn_up(normed, gain, wg, wu, layer, tm=1024, tn=256):
    m, k = normed[0].shape
    n = wg.shape[2]
    return pl.pallas_call(
        _ffn_up_body,
        grid=(m // tm, n // tn),
        in_specs=_normed_specs(tm, k) + [_layer_cols(layer, k, tn, lambda j: j),
                                         _layer_cols(layer, k, tn, lambda j: j)],
        out_specs=pl.BlockSpec((tm, tn), lambda i, j: (i, j)),
        out_shape=jax.ShapeDtypeStruct((m, n), BF16),
        compiler_params=_params("parallel", "arbitrary"),
        name="ffn_up",
    )(*normed, gain, wg, wu)


def _mm_residual_body(a_ref, w_ref, r_ref, o_ref, ob_ref, ss_ref, *, alpha):
    y = r_ref[...] + alpha * _dot(a_ref[...], w_ref[...].astype(BF16))
    o_ref[...] = y
    ob_ref[...] = y.astype(BF16)
    part = _lane_partial_sumsq(y)
    j = pl.program_id(1)

    @pl.when(j == 0)
    def _():
        ss_ref[...] = part

    @pl.when(j > 0)
    def _():
        ss_ref[...] += part


def _mm_residual(a, w, layer, res, alpha, tm=1024, tn=512):
    m, k = a.shape
    n = w.shape[2]
    y, yb, ss = pl.pallas_call(
        functools.partial(_mm_residual_body, alpha=alpha),
        grid=(m // tm, n // tn),
        in_specs=[pl.BlockSpec((tm, k), lambda i, j: (i, 0)),
                  _layer_cols(layer, k, tn, lambda j: j),
                  pl.BlockSpec((tm, tn), lambda i, j: (i, j))],
        out_specs=[pl.BlockSpec((tm, tn), lambda i, j: (i, j)),
                   pl.BlockSpec((tm, tn), lambda i, j: (i, j)),
                   pl.BlockSpec((tm, LANES), lambda i, j: (i, 0))],
        out_shape=[jax.ShapeDtypeStruct((m, n), F32),
                   jax.ShapeDtypeStruct((m, n), BF16),
                   jax.ShapeDtypeStruct((m, LANES), F32)],
        compiler_params=_params("parallel", "arbitrary"),
        name="mm_residual",
    )(a, w, res)
    return y, (yb, ss)


def _mm_cols_body(x_ref, ss_ref, g_ref, w_ref, o_ref):
    x = x_ref[...]
    acc = _row_rsqrt(ss_ref, x.shape[1]) * _dot(x, _gained_bf16(w_ref, g_ref))
    o_ref[...] = acc.astype(o_ref.dtype)


def _mm_cols(normed, gain, w, layer, col_block_of, n_blocks, tm=1024, tn=512):
    m, k = normed[0].shape
    return pl.pallas_call(
        _mm_cols_body,
        grid=(m // tm, n_blocks),
        in_specs=_normed_specs(tm, k) + [_layer_cols(layer, k, tn, col_block_of)],
        out_specs=pl.BlockSpec((tm, tn), lambda i, j: (i, j)),
        out_shape=jax.ShapeDtypeStruct((m, n_blocks * tn), BF16),
        compiler_params=_params("parallel", "arbitrary"),
        name="mm_cols",
    )(*normed, gain, w)


def _mm_rope_body(x_ref, ss_ref, g_ref, w_ref, c_ref, sa_ref, sb_ref, o_ref, *, q_scale,
                  blocks_per_part):
    x = x_ref[...]
    j = pl.program_id(1)
    is_q = (j // blocks_per_part) % 2 == 0
    r = _row_rsqrt(ss_ref, x.shape[1]) * jnp.where(is_q, q_scale, 1.0)
    half = ROT_DIM // 2
    tm, tn = o_ref.shape
    rep = tn // HEAD_DIM
    w = _gained_bf16(w_ref, g_ref)
    sub = tm // 4
    for m0 in range(0, tm, sub):
        rows = slice(m0, m0 + sub)
        c, sa, sb = [jnp.concatenate([t_ref[rows, :] * r[rows]] * rep, axis=1)
                     for t_ref in (c_ref, sa_ref, sb_ref)]
        acc = _dot(x[rows], w)
        lo = pltpu.roll(acc, half, axis=1)
        hi = pltpu.roll(acc, tn - half, axis=1)
        o_ref[rows, :] = (acc * c + lo * sa + hi * sb).astype(o_ref.dtype)


def _mm_rope(normed, gain, w, layer, tables, col_block_of, n_blocks, blocks_per_part, q_scale,
             tm=1024, tn=512):
    m, k = normed[0].shape
    c, sa, sb = tables
    tab_spec = pl.BlockSpec((tm, HEAD_DIM), lambda i, j: (i, 0))
    return pl.pallas_call(
        functools.partial(_mm_rope_body, q_scale=q_scale, blocks_per_part=blocks_per_part),
        grid=(m // tm, n_blocks),
        in_specs=_normed_specs(tm, k) + [_layer_cols(layer, k, tn, col_block_of),
                                         tab_spec, tab_spec, tab_spec],
        out_specs=pl.BlockSpec((tm, tn), lambda i, j: (i, j)),
        out_shape=jax.ShapeDtypeStruct((m, n_blocks * tn), BF16),
        compiler_params=_params("parallel", "arbitrary"),
        name="mm_rope",
    )(*normed, gain, w, c, sa, sb)


def _rope_tables(seq):
    half = ROT_DIM // 2
    inv = ROPE_THETA ** (-jnp.arange(0, ROT_DIM, 2, dtype=F32) / ROT_DIM)
    ang = jnp.arange(seq, dtype=F32)[:, None] * inv[None, :]
    cos, sin = jnp.cos(ang), jnp.sin(ang)
    ones = jnp.ones((seq, HEAD_DIM - ROT_DIM), F32)
    zeros = jnp.zeros((seq, HEAD_DIM - ROT_DIM), F32)
    zh = jnp.zeros((seq, half), F32)
    c = jnp.concatenate([cos, cos, ones], axis=1)
    sa = jnp.concatenate([zh, sin, zeros], axis=1)
    sb = jnp.concatenate([-sin, zh, zeros], axis=1)
    return c, sa, sb


def _diff_attn_body(lam_ref, q_ref, k_ref, v_ref, g_ref, o_ref,
                    s_scr, mpart_scr, m_scr, lpart_scr, acc_scr, *, tu, tk, lam_init):
    tq = q_ref.shape[0]
    nk = k_ref.shape[0] // tk
    nb = tk // LANES
    nr = tq // tu

    def rows(r):
        if isinstance(r, int):
            return slice(r * tu, (r + 1) * tu)
        return pl.ds(pl.multiple_of(r * tu, tu), tu)

    def scores_chunk(r, c, ch):
        q = q_ref[rows(r), c * HEAD_DIM:(c + 1) * HEAD_DIM]
        k = k_ref[pl.ds(pl.multiple_of(ch * tk, tk), tk), c * HEAD_DIM:(c + 1) * HEAD_DIM]
        s = _dot_nt(q, k)
        s_scr[c, ch] = s
        mp = s[:, :LANES]
        for b in range(1, nb):
            mp = jnp.maximum(mp, s[:, b * LANES:(b + 1) * LANES])
        mpart_scr[c] = jnp.maximum(mpart_scr[c], mp)

    def scores_begin(c):
        mpart_scr[c] = jnp.full((tu, LANES), -jnp.inf, F32)

    def scores_end(c):
        m = jnp.max(mpart_scr[c], axis=1, keepdims=True)
        m_scr[c] = jnp.broadcast_to(m, (tu, LANES))

    def values_chunk(c, ch):
        p = jnp.exp2(s_scr[c, ch] - jnp.concatenate([m_scr[c]] * nb, axis=1))
        lp = p[:, :LANES]
        for b in range(1, nb):
            lp = lp + p[:, b * LANES:(b + 1) * LANES]
        lpart_scr[...] += lp
        acc_scr[c] += _dot(p.astype(BF16), v_ref[pl.ds(pl.multiple_of(ch * tk, tk), tk), :])

    def values_begin(c):
        lpart_scr[...] = jnp.zeros((tu, LANES), F32)
        acc_scr[c] = jnp.zeros((tu, 2 * HEAD_DIM), F32)

    def values_end(r, c):
        l = jnp.sum(lpart_scr[...], axis=1, keepdims=True)
        acc_scr[c] = acc_scr[c] / l
        if c == 1:
            lv = lam_ref[...]
            lam = (jnp.exp(jnp.sum(lv[0:1] * lv[1:2], axis=1, keepdims=True))
                   - jnp.exp(jnp.sum(lv[2:3] * lv[3:4], axis=1, keepdims=True)) + lam_init)
            o = acc_scr[0] - lam * acc_scr[1]
            ms = jnp.mean(o * o, axis=-1, keepdims=True)
            o = o * lax.rsqrt(ms + NORM_EPS) * g_ref[...]
            o_ref[rows(r), :] = (o * (1.0 - lam_init)).astype(o_ref.dtype)

    def phase(r_scores, c_scores, r_values, c_values):
        if c_scores is not None:
            scores_begin(c_scores)
        if c_values is not None:
            values_begin(c_values)

        def chunk(ch, carry):
            if c_scores is not None:
                scores_chunk(r_scores, c_scores, ch)
            if c_values is not None:
                values_chunk(c_values, ch)
            return carry

        lax.fori_loop(0, nk, chunk, 0, unroll=min(nk, 4))
        if c_scores is not None:
            scores_end(c_scores)
        if c_values is not None:
            values_end(r_values, c_values)

    phase(0, 0, None, None)

    def row(r, carry):
        phase(r, 1, r, 0)
        phase(r + 1, 0, r, 1)
        return carry

    lax.fori_loop(0, nr - 1, row, 0)
    phase(nr - 1, 1, nr - 1, 0)
    phase(None, None, nr - 1, 1)


def _diff_attn(qk, v, lam_vecs, subln_g, lam_init, q_col0, k_col0, v_col0, tq=4096, tu=256, tk=1024):
    s = qk.shape[0]
    hw = 2 * HEAD_DIM
    return pl.pallas_call(
        functools.partial(_diff_attn_body, tu=tu, tk=tk, lam_init=lam_init),
        grid=(DIFF_HEADS, s // tq),
        in_specs=[pl.BlockSpec((4, HEAD_DIM), lambda h, i: (0, 0)),
                  pl.BlockSpec((tq, hw), lambda h, i: (i, q_col0 + h)),
                  pl.BlockSpec((s, hw), lambda h, i: (0, k_col0 + h)),
                  pl.BlockSpec((s, hw), lambda h, i: (0, v_col0 + h)),
                  pl.BlockSpec((1, hw), lambda h, i: (0, 0))],
        out_specs=pl.BlockSpec((tq, hw), lambda h, i: (i, h)),
        out_shape=jax.ShapeDtypeStruct((s, DIFF_HEADS * hw), BF16),
        scratch_shapes=[pltpu.VMEM((2, s // tk, tu, tk), F32),
                        pltpu.VMEM((2, tu, LANES), F32),
                        pltpu.VMEM((2, tu, LANES), F32),
                        pltpu.VMEM((tu, LANES), F32),
                        pltpu.VMEM((2, tu, hw), F32)],
        compiler_params=_params("parallel", "arbitrary"),
        name="diff_attn",
    )(lam_vecs, qk, qk, v, subln_g.reshape(1, hw))


def _dil_window(i, t, dil, seq):
    reach = DIL_RADIUS * dil
    width = t + 2 * reach
    start = jnp.clip(i * t - reach, 0, seq - width)
    return start, width


def _dil_attn_body(*refs, t, seq):
    n = len(DIL_PATTERNS)
    q_refs, k_refs, v_refs = refs[:n], refs[n:2 * n], refs[2 * n:3 * n]
    o_ref = refs[3 * n]
    bias_refs = refs[3 * n + 1:]
    i = pl.program_id(1)
    outs, lses = [], []
    for g, (_, dil) in enumerate(DIL_PATTERNS):
        start, width = _dil_window(i, t, dil, seq)
        start = pl.multiple_of(start, DIL_RADIUS)
        offset = start - i * t
        prev_start, _ = _dil_window(i - 1, t, dil, seq)
        bias_ref = bias_refs[g]

        @pl.when((i == 0) | (offset != prev_start - (i - 1) * t))
        def _(bias_ref=bias_ref, offset=offset, width=width, dil=dil):
            rel = (offset + lax.broadcasted_iota(jnp.int32, (t, width), 1)
                   - lax.broadcasted_iota(jnp.int32, (t, width), 0))
            off_band = jnp.where(jnp.abs(rel) <= DIL_RADIUS * dil, rel & (dil - 1), 1)
            bias_ref[...] = jnp.where(off_band == 0, 0.0, MASK_VALUE)

        kw = k_refs[g][pl.ds(start, width), :]
        vw = v_refs[g][pl.ds(start, width), :]
        sc = _dot_nt(q_refs[g][...], kw) + bias_ref[...]
        m = jnp.max(sc, axis=-1, keepdims=True)
        p = jnp.exp2(sc - m)
        den = jnp.sum(p, axis=-1, keepdims=True)
        outs.append(_dot(p.astype(BF16), vw) / den)
        lses.append(m + jnp.log2(den))
    top = functools.reduce(jnp.maximum, lses)
    ws = [jnp.exp2(lse - top) for lse in lses]
    total = functools.reduce(lambda a, b: a + b, ws)
    out = functools.reduce(lambda a, b: a + b, [(w / total) * o for w, o in zip(ws, outs)])
    o_ref[...] = out.astype(o_ref.dtype)


def _dil_attn(qk, v, q_col0, k_col0, v_col0, t=256):
    s = qk.shape[0]
    hg = DIL_HEADS_PER_GROUP
    n = len(DIL_PATTERNS)

    def block(col0, g):
        return pl.BlockSpec((t, HEAD_DIM), lambda hh, i: (i, col0 + hg * g + hh))

    def full(col0, g):
        return pl.BlockSpec((s, HEAD_DIM), lambda hh, i: (0, col0 + hg * g + hh))

    widths = [t + 2 * DIL_RADIUS * dil for _, dil in DIL_PATTERNS]
    assert all(window // (2 * dil) == DIL_RADIUS for window, dil in DIL_PATTERNS)
    return pl.pallas_call(
        functools.partial(_dil_attn_body, t=t, seq=s),
        grid=(hg, s // t),
        in_specs=([block(q_col0, g) for g in range(n)] + [full(k_col0, g) for g in range(n)]
                  + [full(v_col0, g) for g in range(n)]),
        out_specs=pl.BlockSpec((t, HEAD_DIM), lambda hh, i: (i, hh)),
        out_shape=jax.ShapeDtypeStruct((s, hg * HEAD_DIM), BF16),
        scratch_shapes=[pltpu.VMEM((t, w), F32) for w in widths],
        compiler_params=_params("arbitrary", "arbitrary"),
        name="dil_attn",
    )(*([qk] * n), *([qk] * n), *([v] * n))


def _row_mask(i, tm, seq):
    row = i * tm - HALO + lax.broadcasted_iota(jnp.int32, (tm + 2 * HALO, 1), 0)
    return jnp.where(row >= 0, jnp.where(row < seq, 1.0, 0.0), 0.0)


def _shift_rows(x, k):
    n = x.shape[0]
    return pltpu.roll(x, (-k) % n, axis=0)


def _conv_body(b_ref, cp_ref, cm_ref, cn_ref, up_ref, um_ref, un_ref, w_ref, o_ref, *, seq):
    tm = b_ref.shape[0]
    i = pl.program_id(0)
    c_ext = jnp.concatenate([cp_ref[...], cm_ref[...], cn_ref[...]], axis=0).astype(F32)
    u_ext = jnp.concatenate([up_ref[...], um_ref[...], un_ref[...]], axis=0).astype(F32)
    z = c_ext * u_ext * _row_mask(i, tm, seq)
    w = w_ref[...]
    y = w[0:1] * _shift_rows(z, -1) + w[1:2] * z + w[2:3] * _shift_rows(z, 1)
    y = y[HALO:HALO + tm]
    o_ref[...] = (b_ref[...].astype(F32) * y).astype(o_ref.dtype)


def _conv_mixer(rest, w_conv, layer, b_col, c_col, u_col, width, tm=512):
    s = rest.shape[0]
    hb = tm // HALO
    n_halo = s // HALO

    def main(col):
        return pl.BlockSpec((tm, width), lambda i: (i, col))

    def prev(col):
        return pl.BlockSpec((HALO, width), lambda i: (jnp.maximum(i * hb - 1, 0), col))

    def nxt(col):
        return pl.BlockSpec((HALO, width), lambda i: (jnp.minimum((i + 1) * hb, n_halo - 1), col))

    return pl.pallas_call(
        functools.partial(_conv_body, seq=s),
        grid=(s // tm,),
        in_specs=[main(b_col), prev(c_col), main(c_col), nxt(c_col),
                  prev(u_col), main(u_col), nxt(u_col),
                  pl.BlockSpec((None,) + w_conv.shape[1:], lambda i: (layer, 0, 0))],
        out_specs=pl.BlockSpec((tm, width), lambda i: (i, 0)),
        out_shape=jax.ShapeDtypeStruct((s, width), BF16),
        compiler_params=_params("parallel"),
        name="conv_mixer",
    )(rest, rest, rest, rest, rest, rest, rest, w_conv)


def _pool_body(up_ref, um_ref, un_ref, w_ref, sc_ref, o_ref, *, seq):
    tm = um_ref.shape[0]
    i = pl.program_id(0)
    ext = jnp.concatenate([up_ref[...], um_ref[...], un_ref[...]], axis=0).astype(F32)
    ext = ext * _row_mask(i, tm, seq)
    pos = i * tm + lax.broadcasted_iota(jnp.int32, (tm, 1), 0)
    gw = ext.shape[1] // len(POOL_WINDOWS)
    for gi, win in enumerate(POOL_WINDOWS):
        sl = slice(gi * gw, (gi + 1) * gw)
        xg = ext[:, sl]
        acc = xg
        span = 1
        while span < win:
            acc = acc + _shift_rows(acc, span)
            span *= 2
        total = _shift_rows(acc, -(win // 2))[HALO:HALO + tm]
        lo = jnp.clip(pos - win // 2, 0, seq)
        hi = jnp.clip(pos - win // 2 + win, 0, seq)
        cnt = (hi - lo).astype(F32)
        pooled = total / cnt - xg[HALO:HALO + tm]
        y = _dot(pooled.astype(BF16), w_ref[gi].astype(BF16))
        o_ref[:, sl] = (y * sc_ref[:, sl]).astype(o_ref.dtype)


def _pool_mixer(rest, w_pool, pool_scale, layer, u_col, width, tm=512):
    s = rest.shape[0]
    hb = tm // HALO
    n_halo = s // HALO
    return pl.pallas_call(
        functools.partial(_pool_body, seq=s),
        grid=(s // tm,),
        in_specs=[pl.BlockSpec((HALO, width), lambda i: (jnp.maximum(i * hb - 1, 0), u_col)),
                  pl.BlockSpec((tm, width), lambda i: (i, u_col)),
                  pl.BlockSpec((HALO, width), lambda i: (jnp.minimum((i + 1) * hb, n_halo - 1), u_col)),
                  pl.BlockSpec((None,) + w_pool.shape[1:], lambda i: (layer, 0, 0, 0)),
                  pl.BlockSpec((1, width), lambda i: (0, 0))],
        out_specs=pl.BlockSpec((tm, width), lambda i: (i, 0)),
        out_shape=jax.ShapeDtypeStruct((s, width), BF16),
        compiler_params=_params("parallel"),
        name="pool_mixer",
    )(rest, rest, rest, w_pool, pool_scale.reshape(1, width))


def _merge_body(ya_ref, yb_ref, yc_ref, yd_ref, gl_ref, wa_ref, wb_ref, wc_ref, wd_ref,
                g0_ref, g1_ref, g2_ref, g3_ref, b0_ref, b1_ref, b2_ref, b3_ref, o_ref):
    gl = gl_ref[...]
    branches = ((ya_ref, wa_ref, g0_ref, b0_ref), (yb_ref, wb_ref, g1_ref, b1_ref),
                (yc_ref, wc_ref, g2_ref, b2_ref), (yd_ref, wd_ref, g3_ref, b3_ref))
    merged = None
    for y_ref, w_ref, g_ref, b_ref in branches:
        gate = jax.nn.sigmoid(_dot(gl, g_ref[...].astype(BF16)) + b_ref[...])
        term = gate * _dot(y_ref[...], w_ref[...].astype(BF16))
        merged = term if merged is None else merged + term
    o_ref[...] = merged.astype(o_ref.dtype)


def _merge(ya, yb, yc, yd, rest, gl_col, w_a, w_b, w_c, w_d, w_gate_up, b_gate, layer,
           tm=1024, tn=256):
    s = ya.shape[0]
    d = w_a.shape[2]
    rank = w_gate_up.shape[1]
    nj = d // tn

    def act(arr):
        return pl.BlockSpec((tm, arr.shape[1]), lambda i, j: (i, 0))

    def wgt(arr):
        return _layer_cols(layer, arr.shape[1], tn, lambda j: j)

    def gate_w(b):
        return _layer_cols(layer, rank, tn, lambda j: b * nj + j)

    def gate_b(b):
        return _layer_cols(layer, 1, tn, lambda j: b * nj + j)

    bias = b_gate.reshape(b_gate.shape[0], 1, N_BRANCHES * d)
    return pl.pallas_call(
        _merge_body,
        grid=(s // tm, nj),
        in_specs=[act(ya), act(yb), act(yc), act(yd),
                  pl.BlockSpec((tm, rank), lambda i, j: (i, gl_col)),
                  wgt(w_a), wgt(w_b), wgt(w_c), wgt(w_d),
                  gate_w(0), gate_w(1), gate_w(2), gate_w(3),
                  gate_b(0), gate_b(1), gate_b(2), gate_b(3)],
        out_specs=pl.BlockSpec((tm, tn), lambda i, j: (i, j)),
        out_shape=jax.ShapeDtypeStruct((s, d), BF16),
        compiler_params=_params("parallel", "arbitrary"),
        name="merge",
    )(ya, yb, yc, yd, rest, w_a, w_b, w_c, w_d,
      w_gate_up, w_gate_up, w_gate_up, w_gate_up, bias, bias, bias, bias)


def kernel(x, ln_ffn1, w1_gate, w1_up, w1_down, ln_mix, w_in, lambda_q1, lambda_k1, lambda_q2, lambda_k2, subln, conv_w, pool_w, pool_scale, w_gate_up, b_gate, w_branch_a, w_branch_b, w_branch_c, w_branch_d, w_out, ln_ffn2, w2_gate, w2_up, w2_down, ln_final):
    b, s, d = x.shape
    depth = ln_ffn1.shape[0]
    mix = 3 * d // 8
    tables = _rope_tables(s)

    tn_rope, tn_cols = 512, 512
    pr, pc = mix // tn_rope, mix // tn_cols
    n_rest = (w_in.shape[2] - 4 * mix) // tn_cols
    hd_per_mix = mix // HEAD_DIM
    q_scale = HEAD_DIM ** -0.5 * math.log2(math.e)

    outs = []
    for bi in range(b):
        xs = x[bi]
        normed = _prep(xs)
        for l in range(depth):
            a = _ffn_up(normed, _lane_replicated(ln_ffn1[l]), w1_gate, w1_up, l)
            xs, normed = _mm_residual(a, w1_down, l, xs, 0.5)

            gain = _lane_replicated(ln_mix[l])
            qk = _mm_rope(normed, gain, w_in, l, tables,
                          lambda j: jnp.where(j < 2 * pr, j, j + pr), 4 * pr, pr, q_scale,
                          tn=tn_rope)
            rest = _mm_cols(normed, gain, w_in, l,
                            lambda j: jnp.where(j < pc, j + 2 * pc, j + 4 * pc), n_rest, tn=tn_cols)

            lam_init = 0.8 - 0.6 * math.exp(-0.3 * l)
            lam_vecs = jnp.stack([lambda_q1[l], lambda_k1[l], lambda_q2[l], lambda_k2[l]])
            ya = _diff_attn(qk, rest, lam_vecs, subln[l], lam_init,
                            q_col0=0, k_col0=DIFF_HEADS, v_col0=0)

            yb = _dil_attn(qk, rest, q_col0=2 * hd_per_mix, k_col0=3 * hd_per_mix,
                           v_col0=hd_per_mix)

            yc = _conv_mixer(rest, conv_w, l, b_col=2, c_col=3, u_col=4, width=mix)
            yd = _pool_mixer(rest, pool_w, pool_scale[l], l, u_col=5, width=mix)

            rank = w_gate_up.shape[1]
            merged = _merge(ya, yb, yc, yd, rest, 6 * mix // rank, w_branch_a, w_branch_b,
                            w_branch_c, w_branch_d, w_gate_up, b_gate, l)
            xs, normed = _mm_residual(merged, w_out, l, xs, 1.0)

            a = _ffn_up(normed, _lane_replicated(ln_ffn2[l]), w2_gate, w2_up, l)
            xs, normed = _mm_residual(a, w2_down, l, xs, 0.5)
        outs.append(_rmsnorm(xs, ln_final, F32))
    return jnp.stack(outs, axis=0)
```

```python
import functools
import math

import jax
import jax.numpy as jnp
from jax import lax
from jax.experimental import pallas as pl
from jax.experimental.pallas import tpu as pltpu

F32 = jnp.float32
BF16 = jnp.bfloat16

LANES = 128
HEAD_DIM = 128
ROT_DIM = HEAD_DIM // 4
ROPE_THETA = 500000.0
NORM_EPS = 1e-5
DIFF_HEADS = 6
DIL_PATTERNS = ((128, 1), (512, 4), (2048, 16))
DIL_HEADS_PER_GROUP = 4
DIL_RADIUS = 64
POOL_WINDOWS = (2, 4, 8, 16)
N_BRANCHES = 4
HALO = 16
MASK_VALUE = -1e30

VMEM_LIMIT_BYTES = 56 * 1024 * 1024


def _params(*sem):
    return pltpu.CompilerParams(dimension_semantics=sem, vmem_limit_bytes=VMEM_LIMIT_BYTES)


def _dot(a, b):
    return jnp.dot(a, b, preferred_element_type=F32)


def _dot_nt(a, b):
    return lax.dot_general(a, b, (((1,), (1,)), ((), ())), preferred_element_type=F32)


def _layer_cols(layer, k, tn, col_of):
    return pl.BlockSpec((None, k, tn), lambda i, j: (layer, 0, col_of(j)))


def _rmsnorm_body(x_ref, g_ref, o_ref):
    x = x_ref[...]
    ms = jnp.mean(x * x, axis=-1, keepdims=True)
    o_ref[...] = (x * lax.rsqrt(ms + NORM_EPS) * g_ref[...]).astype(o_ref.dtype)


def _rmsnorm(x, g, out_dtype, tm=256):
    s, d = x.shape
    return pl.pallas_call(
        _rmsnorm_body,
        grid=(s // tm,),
        in_specs=[pl.BlockSpec((tm, d), lambda i: (i, 0)),
                  pl.BlockSpec((1, d), lambda i: (0, 0))],
        out_specs=pl.BlockSpec((tm, d), lambda i: (i, 0)),
        out_shape=jax.ShapeDtypeStruct((s, d), out_dtype),
        compiler_params=_params("parallel"),
        name="rmsnorm",
    )(x, g.reshape(1, d))


def _lane_partial_sumsq(y):
    acc = None
    for blk in range(y.shape[1] // LANES):
        part = y[:, blk * LANES:(blk + 1) * LANES]
        acc = part * part if acc is None else acc + part * part
    return acc


def _row_rsqrt(ss_ref, width):
    return lax.rsqrt(jnp.sum(ss_ref[...], axis=1, keepdims=True) / width + NORM_EPS)


def _gained_bf16(w_ref, g_ref):
    w = w_ref[...]
    return (w * jnp.concatenate([g_ref[...]] * (w.shape[1] // LANES), axis=1)).astype(BF16)


def _lane_replicated(g):
    return jnp.broadcast_to(g[:, None], (g.shape[0], LANES))


def _prep_body(x_ref, xb_ref, ss_ref):
    x = x_ref[...]
    xb_ref[...] = x.astype(BF16)
    ss_ref[...] = _lane_partial_sumsq(x)


def _prep(x, tm=256):
    s, d = x.shape
    return pl.pallas_call(
        _prep_body,
        grid=(s // tm,),
        in_specs=[pl.BlockSpec((tm, d), lambda i: (i, 0))],
        out_specs=[pl.BlockSpec((tm, d), lambda i: (i, 0)),
                   pl.BlockSpec((tm, LANES), lambda i: (i, 0))],
        out_shape=[jax.ShapeDtypeStruct((s, d), BF16),
                   jax.ShapeDtypeStruct((s, LANES), F32)],
        compiler_params=_params("parallel"),
        name="prep",
    )(x)


def _normed_specs(tm, k):
    return [pl.BlockSpec((tm, k), lambda i, j: (i, 0)),
            pl.BlockSpec((tm, LANES), lambda i, j: (i, 0)),
            pl.BlockSpec((k, LANES), lambda i, j: (0, 0))]


def _slab_cast_specs(w, layer, n_i, n_j):
    rows, cols = w.shape[1:]
    slab = rows // (n_i * n_j)
    assert slab * n_i * n_j == rows and slab % 16 == 0, (rows, n_i, n_j)
    return (pl.BlockSpec((None, slab, cols), lambda i, j: (layer, i * n_j + j, 0)),
            pl.BlockSpec((slab, cols), lambda i, j: (i * n_j + j, 0)),
            jax.ShapeDtypeStruct((rows, cols), BF16))


def _cast_body(w_ref, o_ref):
    o_ref[...] = w_ref[...].astype(BF16)


def _cast_layer(w, layer, n_steps=16):
    cast_in, cast_out, cast_shape = _slab_cast_specs(w, layer, n_steps, 1)
    return pl.pallas_call(
        _cast_body, grid=(n_steps, 1), in_specs=[cast_in], out_specs=cast_out,
        out_shape=cast_shape, compiler_params=_params("parallel", "arbitrary"), name="cast_layer",
    )(w)


def _ffn_up_body(*refs, n_casts):
    x_ref, ss_ref, g_ref, wg_ref, wu_ref = refs[:5]
    cast_in = refs[5:5 + n_casts]
    o_ref = refs[5 + n_casts]
    cast_out = refs[6 + n_casts:6 + 2 * n_casts]
    xg_scr = refs[6 + 2 * n_casts]

    @pl.when(pl.program_id(1) == 0)
    def _():
        xg_scr[...] = (x_ref[...].astype(F32) * g_ref[...]).astype(BF16)

    x = xg_scr[...]
    r = _row_rsqrt(ss_ref, x.shape[1])
    g = r * _dot(x, wg_ref[...])
    u = r * _dot(x, wu_ref[...])
    o_ref[...] = (g * jax.nn.sigmoid(g) * u).astype(o_ref.dtype)
    for src, dst in zip(cast_in, cast_out):
        dst[...] = src[...].astype(BF16)


def _ffn_up(normed, gain, wg, wu, casts, tm=1024, tn=256):
    m, k = normed[0].shape
    n = wg.shape[1]
    n_i, n_j = m // tm, n // tn
    cast_specs = [_slab_cast_specs(w, layer, n_i, n_j) for w, layer in casts]
    outs = pl.pallas_call(
        functools.partial(_ffn_up_body, n_casts=len(casts)),
        grid=(n_i, n_j),
        in_specs=[pl.BlockSpec((tm, k), lambda i, j: (i, 0)),
                  pl.BlockSpec((tm, LANES), lambda i, j: (i, 0)),
                  pl.BlockSpec((1, k), lambda i, j: (0, 0)),
                  pl.BlockSpec((k, tn), lambda i, j: (0, j)),
                  pl.BlockSpec((k, tn), lambda i, j: (0, j))] + [c[0] for c in cast_specs],
        out_specs=[pl.BlockSpec((tm, tn), lambda i, j: (i, j))] + [c[1] for c in cast_specs],
        out_shape=[jax.ShapeDtypeStruct((m, n), BF16)] + [c[2] for c in cast_specs],
        scratch_shapes=[pltpu.VMEM((tm, k), BF16)],
        compiler_params=_params("parallel", "arbitrary"),
        name="ffn_up",
    )(*normed, gain.reshape(1, k), wg, wu, *[w for w, _ in casts])
    return outs[0], outs[1:]


def _mm_residual_body(a_ref, w_ref, r_ref, o_ref, ob_ref, ss_ref, *, alpha):
    y = r_ref[...] + alpha * _dot(a_ref[...], w_ref[...])
    o_ref[...] = y
    ob_ref[...] = y.astype(BF16)
    part = _lane_partial_sumsq(y)
    j = pl.program_id(1)

    @pl.when(j == 0)
    def _():
        ss_ref[...] = part

    @pl.when(j > 0)
    def _():
        ss_ref[...] += part


def _mm_residual(a, w, res, alpha, tm=1024, tn=512):
    m, k = a.shape
    n = w.shape[1]
    y, yb, ss = pl.pallas_call(
        functools.partial(_mm_residual_body, alpha=alpha),
        grid=(m // tm, n // tn),
        in_specs=[pl.BlockSpec((tm, k), lambda i, j: (i, 0)),
                  pl.BlockSpec((k, tn), lambda i, j: (0, j)),
                  pl.BlockSpec((tm, tn), lambda i, j: (i, j))],
        out_specs=[pl.BlockSpec((tm, tn), lambda i, j: (i, j)),
                   pl.BlockSpec((tm, tn), lambda i, j: (i, j)),
                   pl.BlockSpec((tm, LANES), lambda i, j: (i, 0))],
        out_shape=[jax.ShapeDtypeStruct((m, n), F32),
                   jax.ShapeDtypeStruct((m, n), BF16),
                   jax.ShapeDtypeStruct((m, LANES), F32)],
        compiler_params=_params("parallel", "arbitrary"),
        name="mm_residual",
    )(a, w, res)
    return y, (yb, ss)


def _mm_cols_body(x_ref, ss_ref, g_ref, w_ref, o_ref):
    x = x_ref[...]
    acc = _row_rsqrt(ss_ref, x.shape[1]) * _dot(x, _gained_bf16(w_ref, g_ref))
    o_ref[...] = acc.astype(o_ref.dtype)


def _mm_cols(normed, gain, w, layer, col_block_of, n_blocks, tm=1024, tn=512):
    m, k = normed[0].shape
    return pl.pallas_call(
        _mm_cols_body,
        grid=(m // tm, n_blocks),
        in_specs=_normed_specs(tm, k) + [_layer_cols(layer, k, tn, col_block_of)],
        out_specs=pl.BlockSpec((tm, tn), lambda i, j: (i, j)),
        out_shape=jax.ShapeDtypeStruct((m, n_blocks * tn), BF16),
        compiler_params=_params("parallel", "arbitrary"),
        name="mm_cols",
    )(*normed, gain, w)


def _mm_rope_body(x_ref, ss_ref, g_ref, w_ref, c_ref, sa_ref, sb_ref, o_ref, *, q_scale,
                  blocks_per_part):
    x = x_ref[...]
    j = pl.program_id(1)
    is_q = (j // blocks_per_part) % 2 == 0
    r = _row_rsqrt(ss_ref, x.shape[1]) * jnp.where(is_q, q_scale, 1.0)
    half = ROT_DIM // 2
    tm, tn = o_ref.shape
    rep = tn // HEAD_DIM
    w = _gained_bf16(w_ref, g_ref)
    sub = tm // 4
    for m0 in range(0, tm, sub):
        rows = slice(m0, m0 + sub)
        c, sa, sb = [jnp.concatenate([t_ref[rows, :] * r[rows]] * rep, axis=1)
                     for t_ref in (c_ref, sa_ref, sb_ref)]
        acc = _dot(x[rows], w)
        lo = pltpu.roll(acc, half, axis=1)
        hi = pltpu.roll(acc, tn - half, axis=1)
        o_ref[rows, :] = (acc * c + lo * sa + hi * sb).astype(o_ref.dtype)


def _mm_rope(normed, gain, w, layer, tables, col_block_of, n_blocks, blocks_per_part, q_scale,
             tm=1024, tn=512):
    m, k = normed[0].shape
    c, sa, sb = tables
    tab_spec = pl.BlockSpec((tm, HEAD_DIM), lambda i, j: (i, 0))
    return pl.pallas_call(
        functools.partial(_mm_rope_body, q_scale=q_scale, blocks_per_part=blocks_per_part),
        grid=(m // tm, n_blocks),
        in_specs=_normed_specs(tm, k) + [_layer_cols(layer, k, tn, col_block_of),
                                         tab_spec, tab_spec, tab_spec],
        out_specs=pl.BlockSpec((tm, tn), lambda i, j: (i, j)),
        out_shape=jax.ShapeDtypeStruct((m, n_blocks * tn), BF16),
        compiler_params=_params("parallel", "arbitrary"),
        name="mm_rope",
    )(*normed, gain, w, c, sa, sb)


def _rope_tables(seq):
    half = ROT_DIM // 2
    inv = ROPE_THETA ** (-jnp.arange(0, ROT_DIM, 2, dtype=F32) / ROT_DIM)
    ang = jnp.arange(seq, dtype=F32)[:, None] * inv[None, :]
    cos, sin = jnp.cos(ang), jnp.sin(ang)
    ones = jnp.ones((seq, HEAD_DIM - ROT_DIM), F32)
    zeros = jnp.zeros((seq, HEAD_DIM - ROT_DIM), F32)
    zh = jnp.zeros((seq, half), F32)
    c = jnp.concatenate([cos, cos, ones], axis=1)
    sa = jnp.concatenate([zh, sin, zeros], axis=1)
    sb = jnp.concatenate([-sin, zh, zeros], axis=1)
    return c, sa, sb


def _diff_attn_body(lam_ref, q_ref, k_ref, v_ref, g_ref, o_ref,
                    s_scr, mpart_scr, m_scr, lpart_scr, acc_scr, *, tu, tk, lam_init):
    tq = q_ref.shape[0]
    nk = k_ref.shape[0] // tk
    nb = tk // LANES
    nr = tq // tu

    def rows(r):
        if isinstance(r, int):
            return slice(r * tu, (r + 1) * tu)
        return pl.ds(pl.multiple_of(r * tu, tu), tu)

    def scores_chunk(r, c, ch):
        q = q_ref[rows(r), c * HEAD_DIM:(c + 1) * HEAD_DIM]
        k = k_ref[pl.ds(pl.multiple_of(ch * tk, tk), tk), c * HEAD_DIM:(c + 1) * HEAD_DIM]
        s = _dot_nt(q, k)
        s_scr[c, ch] = s
        mp = s[:, :LANES]
        for b in range(1, nb):
            mp = jnp.maximum(mp, s[:, b * LANES:(b + 1) * LANES])
        mpart_scr[c] = jnp.maximum(mpart_scr[c], mp)

    def scores_begin(c):
        mpart_scr[c] = jnp.full((tu, LANES), -jnp.inf, F32)

    def scores_end(c):
        m = jnp.max(mpart_scr[c], axis=1, keepdims=True)
        m_scr[c] = jnp.broadcast_to(m, (tu, LANES))

    def values_chunk(c, ch):
        p = jnp.exp2(s_scr[c, ch] - jnp.concatenate([m_scr[c]] * nb, axis=1))
        lp = p[:, :LANES]
        for b in range(1, nb):
            lp = lp + p[:, b * LANES:(b + 1) * LANES]
        lpart_scr[...] += lp
        acc_scr[c] += _dot(p.astype(BF16), v_ref[pl.ds(pl.multiple_of(ch * tk, tk), tk), :])

    def values_begin(c):
        lpart_scr[...] = jnp.zeros((tu, LANES), F32)
        acc_scr[c] = jnp.zeros((tu, 2 * HEAD_DIM), F32)

    def values_end(r, c):
        l = jnp.sum(lpart_scr[...], axis=1, keepdims=True)
        acc_scr[c] = acc_scr[c] / l
        if c == 1:
            lv = lam_ref[...]
            lam = (jnp.exp(jnp.sum(lv[0:1] * lv[1:2], axis=1, keepdims=True))
                   - jnp.exp(jnp.sum(lv[2:3] * lv[3:4], axis=1, keepdims=True)) + lam_init)
            o = acc_scr[0] - lam * acc_scr[1]
            ms = jnp.mean(o * o, axis=-1, keepdims=True)
            o = o * lax.rsqrt(ms + NORM_EPS) * g_ref[...]
            o_ref[rows(r), :] = (o * (1.0 - lam_init)).astype(o_ref.dtype)

    def phase(r_scores, c_scores, r_values, c_values):
        if c_scores is not None:
            scores_begin(c_scores)
        if c_values is not None:
            values_begin(c_values)

        def chunk(ch, carry):
            if c_scores is not None:
                scores_chunk(r_scores, c_scores, ch)
            if c_values is not None:
                values_chunk(c_values, ch)
            return carry

        lax.fori_loop(0, nk, chunk, 0, unroll=min(nk, 4))
        if c_scores is not None:
            scores_end(c_scores)
        if c_values is not None:
            values_end(r_values, c_values)

    phase(0, 0, None, None)

    def row(r, carry):
        phase(r, 1, r, 0)
        phase(r + 1, 0, r, 1)
        return carry

    lax.fori_loop(0, nr - 1, row, 0)
    phase(nr - 1, 1, nr - 1, 0)
    phase(None, None, nr - 1, 1)


def _diff_attn(qk, v, lam_vecs, subln_g, lam_init, q_col0, k_col0, v_col0, tq=4096, tu=256, tk=1024):
    s = qk.shape[0]
    hw = 2 * HEAD_DIM
    return pl.pallas_call(
        functools.partial(_diff_attn_body, tu=tu, tk=tk, lam_init=lam_init),
        grid=(DIFF_HEADS, s // tq),
        in_specs=[pl.BlockSpec((4, HEAD_DIM), lambda h, i: (0, 0)),
                  pl.BlockSpec((tq, hw), lambda h, i: (i, q_col0 + h)),
                  pl.BlockSpec((s, hw), lambda h, i: (0, k_col0 + h)),
                  pl.BlockSpec((s, hw), lambda h, i: (0, v_col0 + h)),
                  pl.BlockSpec((1, hw), lambda h, i: (0, 0))],
        out_specs=pl.BlockSpec((tq, hw), lambda h, i: (i, h)),
        out_shape=jax.ShapeDtypeStruct((s, DIFF_HEADS * hw), BF16),
        scratch_shapes=[pltpu.VMEM((2, s // tk, tu, tk), F32),
                        pltpu.VMEM((2, tu, LANES), F32),
                        pltpu.VMEM((2, tu, LANES), F32),
                        pltpu.VMEM((tu, LANES), F32),
                        pltpu.VMEM((2, tu, hw), F32)],
        compiler_params=_params("parallel", "arbitrary"),
        name="diff_attn",
    )(lam_vecs, qk, qk, v, subln_g.reshape(1, hw))


def _dil_window(i, t, dil, seq):
    reach = DIL_RADIUS * dil
    width = t + 2 * reach
    start = jnp.clip(i * t - reach, 0, seq - width)
    return start, width


def _dil_attn_body(*refs, t, seq):
    n = len(DIL_PATTERNS)
    q_refs, k_refs, v_refs = refs[:n], refs[n:2 * n], refs[2 * n:3 * n]
    o_ref = refs[3 * n]
    bias_refs = refs[3 * n + 1:]
    i = pl.program_id(1)
    outs, lses = [], []
    for g, (_, dil) in enumerate(DIL_PATTERNS):
        start, width = _dil_window(i, t, dil, seq)
        start = pl.multiple_of(start, DIL_RADIUS)
        offset = start - i * t
        prev_start, _ = _dil_window(i - 1, t, dil, seq)
        bias_ref = bias_refs[g]

        @pl.when((i == 0) | (offset != prev_start - (i - 1) * t))
        def _(bias_ref=bias_ref, offset=offset, width=width, dil=dil):
            rel = (offset + lax.broadcasted_iota(jnp.int32, (t, width), 1)
                   - lax.broadcasted_iota(jnp.int32, (t, width), 0))
            off_band = jnp.where(jnp.abs(rel) <= DIL_RADIUS * dil, rel & (dil - 1), 1)
            bias_ref[...] = jnp.where(off_band == 0, 0.0, MASK_VALUE)

        kw = k_refs[g][pl.ds(start, width), :]
        vw = v_refs[g][pl.ds(start, width), :]
        sc = _dot_nt(q_refs[g][...], kw) + bias_ref[...]
        m = jnp.max(sc, axis=-1, keepdims=True)
        p = jnp.exp2(sc - m)
        den = jnp.sum(p, axis=-1, keepdims=True)
        outs.append(_dot(p.astype(BF16), vw) / den)
        lses.append(m + jnp.log2(den))
    top = functools.reduce(jnp.maximum, lses)
    ws = [jnp.exp2(lse - top) for lse in lses]
    total = functools.reduce(lambda a, b: a + b, ws)
    out = functools.reduce(lambda a, b: a + b, [(w / total) * o for w, o in zip(ws, outs)])
    o_ref[...] = out.astype(o_ref.dtype)


def _dil_attn(qk, v, q_col0, k_col0, v_col0, t=256):
    s = qk.shape[0]
    hg = DIL_HEADS_PER_GROUP
    n = len(DIL_PATTERNS)

    def block(col0, g):
        return pl.BlockSpec((t, HEAD_DIM), lambda hh, i: (i, col0 + hg * g + hh))

    def full(col0, g):
        return pl.BlockSpec((s, HEAD_DIM), lambda hh, i: (0, col0 + hg * g + hh))

    widths = [t + 2 * DIL_RADIUS * dil for _, dil in DIL_PATTERNS]
    assert all(window // (2 * dil) == DIL_RADIUS for window, dil in DIL_PATTERNS)
    return pl.pallas_call(
        functools.partial(_dil_attn_body, t=t, seq=s),
        grid=(hg, s // t),
        in_specs=([block(q_col0, g) for g in range(n)] + [full(k_col0, g) for g in range(n)]
                  + [full(v_col0, g) for g in range(n)]),
        out_specs=pl.BlockSpec((t, HEAD_DIM), lambda hh, i: (i, hh)),
        out_shape=jax.ShapeDtypeStruct((s, hg * HEAD_DIM), BF16),
        scratch_shapes=[pltpu.VMEM((t, w), F32) for w in widths],
        compiler_params=_params("arbitrary", "arbitrary"),
        name="dil_attn",
    )(*([qk] * n), *([qk] * n), *([v] * n))


def _row_mask(i, tm, seq):
    row = i * tm - HALO + lax.broadcasted_iota(jnp.int32, (tm + 2 * HALO, 1), 0)
    return jnp.where(row >= 0, jnp.where(row < seq, 1.0, 0.0), 0.0)


def _shift_rows(x, k):
    n = x.shape[0]
    return pltpu.roll(x, (-k) % n, axis=0)


def _conv_body(b_ref, cp_ref, cm_ref, cn_ref, up_ref, um_ref, un_ref, w_ref, o_ref, *, seq):
    tm = b_ref.shape[0]
    i = pl.program_id(0)
    c_ext = jnp.concatenate([cp_ref[...], cm_ref[...], cn_ref[...]], axis=0).astype(F32)
    u_ext = jnp.concatenate([up_ref[...], um_ref[...], un_ref[...]], axis=0).astype(F32)
    z = c_ext * u_ext * _row_mask(i, tm, seq)
    w = w_ref[...]
    y = w[0:1] * _shift_rows(z, -1) + w[1:2] * z + w[2:3] * _shift_rows(z, 1)
    y = y[HALO:HALO + tm]
    o_ref[...] = (b_ref[...].astype(F32) * y).astype(o_ref.dtype)


def _conv_mixer(rest, w_conv, layer, b_col, c_col, u_col, width, tm=512):
    s = rest.shape[0]
    hb = tm // HALO
    n_halo = s // HALO

    def main(col):
        return pl.BlockSpec((tm, width), lambda i: (i, col))

    def prev(col):
        return pl.BlockSpec((HALO, width), lambda i: (jnp.maximum(i * hb - 1, 0), col))

    def nxt(col):
        return pl.BlockSpec((HALO, width), lambda i: (jnp.minimum((i + 1) * hb, n_halo - 1), col))

    return pl.pallas_call(
        functools.partial(_conv_body, seq=s),
        grid=(s // tm,),
        in_specs=[main(b_col), prev(c_col), main(c_col), nxt(c_col),
                  prev(u_col), main(u_col), nxt(u_col),
                  pl.BlockSpec((None,) + w_conv.shape[1:], lambda i: (layer, 0, 0))],
        out_specs=pl.BlockSpec((tm, width), lambda i: (i, 0)),
        out_shape=jax.ShapeDtypeStruct((s, width), BF16),
        compiler_params=_params("parallel"),
        name="conv_mixer",
    )(rest, rest, rest, rest, rest, rest, rest, w_conv)


def _pool_body(up_ref, um_ref, un_ref, w_ref, sc_ref, o_ref, *, seq):
    tm = um_ref.shape[0]
    i = pl.program_id(0)
    ext = jnp.concatenate([up_ref[...], um_ref[...], un_ref[...]], axis=0).astype(F32)
    ext = ext * _row_mask(i, tm, seq)
    pos = i * tm + lax.broadcasted_iota(jnp.int32, (tm, 1), 0)
    gw = ext.shape[1] // len(POOL_WINDOWS)
    for gi, win in enumerate(POOL_WINDOWS):
        sl = slice(gi * gw, (gi + 1) * gw)
        xg = ext[:, sl]
        acc = xg
        span = 1
        while span < win:
            acc = acc + _shift_rows(acc, span)
            span *= 2
        total = _shift_rows(acc, -(win // 2))[HALO:HALO + tm]
        lo = jnp.clip(pos - win // 2, 0, seq)
        hi = jnp.clip(pos - win // 2 + win, 0, seq)
        cnt = (hi - lo).astype(F32)
        pooled = total / cnt - xg[HALO:HALO + tm]
        y = _dot(pooled.astype(BF16), w_ref[gi].astype(BF16))
        o_ref[:, sl] = (y * sc_ref[:, sl]).astype(o_ref.dtype)


def _pool_mixer(rest, w_pool, pool_scale, layer, u_col, width, tm=512):
    s = rest.shape[0]
    hb = tm // HALO
    n_halo = s // HALO
    return pl.pallas_call(
        functools.partial(_pool_body, seq=s),
        grid=(s // tm,),
        in_specs=[pl.BlockSpec((HALO, width), lambda i: (jnp.maximum(i * hb - 1, 0), u_col)),
                  pl.BlockSpec((tm, width), lambda i: (i, u_col)),
                  pl.BlockSpec((HALO, width), lambda i: (jnp.minimum((i + 1) * hb, n_halo - 1), u_col)),
                  pl.BlockSpec((None,) + w_pool.shape[1:], lambda i: (layer, 0, 0, 0)),
                  pl.BlockSpec((1, width), lambda i: (0, 0))],
        out_specs=pl.BlockSpec((tm, width), lambda i: (i, 0)),
        out_shape=jax.ShapeDtypeStruct((s, width), BF16),
        compiler_params=_params("parallel"),
        name="pool_mixer",
    )(rest, rest, rest, w_pool, pool_scale.reshape(1, width))


def _merge_body(ya_ref, yb_ref, yc_ref, yd_ref, gl_ref, wa_ref, wb_ref, wc_ref, wd_ref,
                g0_ref, g1_ref, g2_ref, g3_ref, b0_ref, b1_ref, b2_ref, b3_ref, wn_ref,
                o_ref, wnb_ref):
    wnb_ref[...] = wn_ref[...].astype(BF16)
    gl = gl_ref[...]
    branches = ((ya_ref, wa_ref, g0_ref, b0_ref), (yb_ref, wb_ref, g1_ref, b1_ref),
                (yc_ref, wc_ref, g2_ref, b2_ref), (yd_ref, wd_ref, g3_ref, b3_ref))
    merged = None
    for y_ref, w_ref, g_ref, b_ref in branches:
        gate = jax.nn.sigmoid(_dot(gl, g_ref[...].astype(BF16)) + b_ref[...])
        term = gate * _dot(y_ref[...], w_ref[...].astype(BF16))
        merged = term if merged is None else merged + term
    o_ref[...] = merged.astype(o_ref.dtype)


def _merge(ya, yb, yc, yd, rest, gl_col, w_a, w_b, w_c, w_d, w_gate_up, b_gate, w_next, layer,
           tm=1024, tn=256):
    s = ya.shape[0]
    d = w_a.shape[2]
    rank = w_gate_up.shape[1]
    nj = d // tn

    def act(arr):
        return pl.BlockSpec((tm, arr.shape[1]), lambda i, j: (i, 0))

    def wgt(arr):
        return _layer_cols(layer, arr.shape[1], tn, lambda j: j)

    def gate_w(b):
        return _layer_cols(layer, rank, tn, lambda j: b * nj + j)

    def gate_b(b):
        return _layer_cols(layer, 1, tn, lambda j: b * nj + j)

    bias = b_gate.reshape(b_gate.shape[0], 1, N_BRANCHES * d)
    cast_in, cast_out, cast_shape = _slab_cast_specs(w_next, layer, s // tm, nj)
    return pl.pallas_call(
        _merge_body,
        grid=(s // tm, nj),
        in_specs=[act(ya), act(yb), act(yc), act(yd),
                  pl.BlockSpec((tm, rank), lambda i, j: (i, gl_col)),
                  wgt(w_a), wgt(w_b), wgt(w_c), wgt(w_d),
                  gate_w(0), gate_w(1), gate_w(2), gate_w(3),
                  gate_b(0), gate_b(1), gate_b(2), gate_b(3), cast_in],
        out_specs=[pl.BlockSpec((tm, tn), lambda i, j: (i, j)), cast_out],
        out_shape=[jax.ShapeDtypeStruct((s, d), BF16), cast_shape],
        compiler_params=_params("parallel", "arbitrary"),
        name="merge",
    )(ya, yb, yc, yd, rest, w_a, w_b, w_c, w_d,
      w_gate_up, w_gate_up, w_gate_up, w_gate_up, bias, bias, bias, bias, w_next)


def kernel(x, ln_ffn1, w1_gate, w1_up, w1_down, ln_mix, w_in, lambda_q1, lambda_k1, lambda_q2, lambda_k2, subln, conv_w, pool_w, pool_scale, w_gate_up, b_gate, w_branch_a, w_branch_b, w_branch_c, w_branch_d, w_out, ln_ffn2, w2_gate, w2_up, w2_down, ln_final):
    b, s, d = x.shape
    depth = ln_ffn1.shape[0]
    mix = 3 * d // 8
    tables = _rope_tables(s)

    tn_rope, tn_cols = 512, 512
    pr, pc = mix // tn_rope, mix // tn_cols
    n_rest = (w_in.shape[2] - 4 * mix) // tn_cols
    hd_per_mix = mix // HEAD_DIM
    q_scale = HEAD_DIM ** -0.5 * math.log2(math.e)

    outs = []
    for bi in range(b):
        xs = x[bi]
        normed = _prep(xs)
        ffn1_w = (_cast_layer(w1_gate, 0), _cast_layer(w1_up, 0))
        for l in range(depth):
            a, (w_down, *ffn2_w) = _ffn_up(normed, ln_ffn1[l], *ffn1_w,
                                           [(w1_down, l), (w2_gate, l), (w2_up, l)])
            xs, normed = _mm_residual(a, w_down, xs, 0.5)

            gain = _lane_replicated(ln_mix[l])
            qk = _mm_rope(normed, gain, w_in, l, tables,
                          lambda j: jnp.where(j < 2 * pr, j, j + pr), 4 * pr, pr, q_scale,
                          tn=tn_rope)
            rest = _mm_cols(normed, gain, w_in, l,
                            lambda j: jnp.where(j < pc, j + 2 * pc, j + 4 * pc), n_rest, tn=tn_cols)

            lam_init = 0.8 - 0.6 * math.exp(-0.3 * l)
            lam_vecs = jnp.stack([lambda_q1[l], lambda_k1[l], lambda_q2[l], lambda_k2[l]])
            ya = _diff_attn(qk, rest, lam_vecs, subln[l], lam_init,
                            q_col0=0, k_col0=DIFF_HEADS, v_col0=0)

            yb = _dil_attn(qk, rest, q_col0=2 * hd_per_mix, k_col0=3 * hd_per_mix,
                           v_col0=hd_per_mix)

            yc = _conv_mixer(rest, conv_w, l, b_col=2, c_col=3, u_col=4, width=mix)
            yd = _pool_mixer(rest, pool_w, pool_scale[l], l, u_col=5, width=mix)

            rank = w_gate_up.shape[1]
            merged, w_out_l = _merge(ya, yb, yc, yd, rest, 6 * mix // rank, w_branch_a,
                                     w_branch_b, w_branch_c, w_branch_d, w_gate_up, b_gate,
                                     w_out, l)
            xs, normed = _mm_residual(merged, w_out_l, xs, 1.0)

            ahead = [(w1_gate, l + 1), (w1_up, l + 1)] if l + 1 < depth else []
            a, (w_down, *ffn1_w) = _ffn_up(normed, ln_ffn2[l], *ffn2_w, [(w2_down, l)] + ahead)
            xs, normed = _mm_residual(a, w_down, xs, 0.5)
        outs.append(_rmsnorm(xs, ln_final, F32))
    return jnp.stack(outs, axis=0)
```

```python
import functools
import math

import jax
import jax.numpy as jnp
from jax import lax
from jax.experimental import pallas as pl
from jax.experimental.pallas import tpu as pltpu

F32 = jnp.float32
BF16 = jnp.bfloat16

LANES = 128
HEAD_DIM = 128
ROT_DIM = HEAD_DIM // 4
ROPE_THETA = 500000.0
NORM_EPS = 1e-5
DIFF_HEADS = 6
DIL_PATTERNS = ((128, 1), (512, 4), (2048, 16))
DIL_HEADS_PER_GROUP = 4
DIL_RADIUS = 64
POOL_WINDOWS = (2, 4, 8, 16)
N_BRANCHES = 4
HALO = 16
MASK_VALUE = -1e30

VMEM_LIMIT_BYTES = 56 * 1024 * 1024


def _params(*sem):
    return pltpu.CompilerParams(dimension_semantics=sem, vmem_limit_bytes=VMEM_LIMIT_BYTES)


def _dot(a, b):
    return jnp.dot(a, b, preferred_element_type=F32)


def _dot_nt(a, b):
    return lax.dot_general(a, b, (((1,), (1,)), ((), ())), preferred_element_type=F32)


def _layer_cols(layer, k, tn, col_of):
    return pl.BlockSpec((None, k, tn), lambda i, j: (layer, 0, col_of(j)))


def _rmsnorm_body(x_ref, g_ref, o_ref):
    x = x_ref[...]
    ms = jnp.mean(x * x, axis=-1, keepdims=True)
    o_ref[...] = (x * lax.rsqrt(ms + NORM_EPS) * g_ref[...]).astype(o_ref.dtype)


def _rmsnorm(x, g, out_dtype, tm=256):
    s, d = x.shape
    return pl.pallas_call(
        _rmsnorm_body,
        grid=(s // tm,),
        in_specs=[pl.BlockSpec((tm, d), lambda i: (i, 0)),
                  pl.BlockSpec((1, d), lambda i: (0, 0))],
        out_specs=pl.BlockSpec((tm, d), lambda i: (i, 0)),
        out_shape=jax.ShapeDtypeStruct((s, d), out_dtype),
        compiler_params=_params("parallel"),
        name="rmsnorm",
    )(x, g.reshape(1, d))


def _lane_partial_sumsq(y):
    acc = None
    for blk in range(y.shape[1] // LANES):
        part = y[:, blk * LANES:(blk + 1) * LANES]
        acc = part * part if acc is None else acc + part * part
    return acc


def _row_rsqrt(ss_ref, width):
    return lax.rsqrt(jnp.sum(ss_ref[...], axis=1, keepdims=True) / width + NORM_EPS)


def _gained_bf16(w_ref, g_ref):
    w = w_ref[...]
    return (w * jnp.concatenate([g_ref[...]] * (w.shape[1] // LANES), axis=1)).astype(BF16)


def _lane_replicated(g):
    return jnp.broadcast_to(g[:, None], (g.shape[0], LANES))


def _prep_body(x_ref, xb_ref, ss_ref):
    x = x_ref[...]
    xb_ref[...] = x.astype(BF16)
    ss_ref[...] = _lane_partial_sumsq(x)


def _prep(x, tm=256):
    s, d = x.shape
    return pl.pallas_call(
        _prep_body,
        grid=(s // tm,),
        in_specs=[pl.BlockSpec((tm, d), lambda i: (i, 0))],
        out_specs=[pl.BlockSpec((tm, d), lambda i: (i, 0)),
                   pl.BlockSpec((tm, LANES), lambda i: (i, 0))],
        out_shape=[jax.ShapeDtypeStruct((s, d), BF16),
                   jax.ShapeDtypeStruct((s, LANES), F32)],
        compiler_params=_params("parallel"),
        name="prep",
    )(x)


def _normed_specs(tm, k):
    return [pl.BlockSpec((tm, k), lambda i, j: (i, 0)),
            pl.BlockSpec((tm, LANES), lambda i, j: (i, 0)),
            pl.BlockSpec((k, LANES), lambda i, j: (0, 0))]


class _SlabCasts:
    def __init__(self, jobs, n_i, n_j):
        self.in_specs, self.out_specs, self.out_shapes, self.operands, self.has_gain = [], [], [], [], []
        for w, layer, gain in jobs:
            rows, cols = w.shape[1:]
            slab = rows // (n_i * n_j)
            assert slab * n_i * n_j == rows and slab % 16 == 0, (rows, n_i, n_j)
            self.in_specs.append(pl.BlockSpec((None, slab, cols),
                                              lambda i, j, layer=layer: (layer, i * n_j + j, 0)))
            self.operands.append(w)
            if gain is not None:
                self.in_specs.append(pl.BlockSpec((slab, LANES), lambda i, j: (i * n_j + j, 0)))
                self.operands.append(_lane_replicated(gain))
            self.has_gain.append(gain is not None)
            self.out_specs.append(pl.BlockSpec((slab, cols), lambda i, j: (i * n_j + j, 0)))
            self.out_shapes.append(jax.ShapeDtypeStruct((rows, cols), BF16))

    def run(self, in_refs, out_refs):
        in_refs = list(in_refs)
        for has_gain, dst in zip(self.has_gain, out_refs):
            w = in_refs.pop(0)[...]
            if has_gain:
                w = w * jnp.concatenate([in_refs.pop(0)[...]] * (w.shape[1] // LANES), axis=1)
            dst[...] = w.astype(BF16)


def _cast_layer(w, layer, gain, n_steps=16):
    casts = _SlabCasts([(w, layer, gain)], n_steps, 1)

    def body(*refs):
        casts.run(refs[:-1], refs[-1:])

    return pl.pallas_call(
        body, grid=(n_steps, 1), in_specs=casts.in_specs, out_specs=casts.out_specs[0],
        out_shape=casts.out_shapes[0], compiler_params=_params("parallel", "arbitrary"),
        name="cast_layer",
    )(*casts.operands)


def _ffn_up_body(*refs, casts):
    n_in, n_out = len(casts.in_specs), len(casts.out_specs)
    x_ref, ss_ref, wg_ref, wu_ref = refs[:4]
    o_ref = refs[4 + n_in]
    x = x_ref[...]
    r = _row_rsqrt(ss_ref, x.shape[1])
    g = r * _dot(x, wg_ref[...])
    u = r * _dot(x, wu_ref[...])
    o_ref[...] = (g * jax.nn.sigmoid(g) * u).astype(o_ref.dtype)
    casts.run(refs[4:4 + n_in], refs[5 + n_in:5 + n_in + n_out])


def _ffn_up(normed, wg, wu, cast_jobs, tm=1024, tn=512):
    m, k = normed[0].shape
    n = wg.shape[1]
    n_i, n_j = m // tm, n // tn
    casts = _SlabCasts(cast_jobs, n_i, n_j)
    outs = pl.pallas_call(
        functools.partial(_ffn_up_body, casts=casts),
        grid=(n_i, n_j),
        in_specs=[pl.BlockSpec((tm, k), lambda i, j: (i, 0)),
                  pl.BlockSpec((tm, LANES), lambda i, j: (i, 0)),
                  pl.BlockSpec((k, tn), lambda i, j: (0, j)),
                  pl.BlockSpec((k, tn), lambda i, j: (0, j))] + casts.in_specs,
        out_specs=[pl.BlockSpec((tm, tn), lambda i, j: (i, j))] + casts.out_specs,
        out_shape=[jax.ShapeDtypeStruct((m, n), BF16)] + casts.out_shapes,
        compiler_params=_params("parallel", "arbitrary"),
        name="ffn_up",
    )(*normed, wg, wu, *casts.operands)
    return outs[0], outs[1:]


def _mm_residual_body(a_ref, w_ref, r_ref, o_ref, ob_ref, ss_ref, *, alpha):
    y = r_ref[...] + alpha * _dot(a_ref[...], w_ref[...])
    o_ref[...] = y
    ob_ref[...] = y.astype(BF16)
    part = _lane_partial_sumsq(y)
    j = pl.program_id(1)

    @pl.when(j == 0)
    def _():
        ss_ref[...] = part

    @pl.when(j > 0)
    def _():
        ss_ref[...] += part


def _mm_residual(a, w, res, alpha, tm=1024, tn=512):
    m, k = a.shape
    n = w.shape[1]
    y, yb, ss = pl.pallas_call(
        functools.partial(_mm_residual_body, alpha=alpha),
        grid=(m // tm, n // tn),
        in_specs=[pl.BlockSpec((tm, k), lambda i, j: (i, 0)),
                  pl.BlockSpec((k, tn), lambda i, j: (0, j)),
                  pl.BlockSpec((tm, tn), lambda i, j: (i, j))],
        out_specs=[pl.BlockSpec((tm, tn), lambda i, j: (i, j)),
                   pl.BlockSpec((tm, tn), lambda i, j: (i, j)),
                   pl.BlockSpec((tm, LANES), lambda i, j: (i, 0))],
        out_shape=[jax.ShapeDtypeStruct((m, n), F32),
                   jax.ShapeDtypeStruct((m, n), BF16),
                   jax.ShapeDtypeStruct((m, LANES), F32)],
        compiler_params=_params("parallel", "arbitrary"),
        name="mm_residual",
    )(a, w, res)
    return y, (yb, ss)


def _mm_cols_body(x_ref, ss_ref, g_ref, w_ref, o_ref):
    x = x_ref[...]
    acc = _row_rsqrt(ss_ref, x.shape[1]) * _dot(x, _gained_bf16(w_ref, g_ref))
    o_ref[...] = acc.astype(o_ref.dtype)


def _mm_cols(normed, gain, w, layer, col_block_of, n_blocks, tm=1024, tn=512):
    m, k = normed[0].shape
    return pl.pallas_call(
        _mm_cols_body,
        grid=(m // tm, n_blocks),
        in_specs=_normed_specs(tm, k) + [_layer_cols(layer, k, tn, col_block_of)],
        out_specs=pl.BlockSpec((tm, tn), lambda i, j: (i, j)),
        out_shape=jax.ShapeDtypeStruct((m, n_blocks * tn), BF16),
        compiler_params=_params("parallel", "arbitrary"),
        name="mm_cols",
    )(*normed, gain, w)


def _mm_rope_body(x_ref, ss_ref, g_ref, w_ref, c_ref, sa_ref, sb_ref, o_ref, *, q_scale,
                  blocks_per_part):
    x = x_ref[...]
    j = pl.program_id(1)
    is_q = (j // blocks_per_part) % 2 == 0
    r = _row_rsqrt(ss_ref, x.shape[1]) * jnp.where(is_q, q_scale, 1.0)
    half = ROT_DIM // 2
    tm, tn = o_ref.shape
    rep = tn // HEAD_DIM
    w = _gained_bf16(w_ref, g_ref)
    sub = tm // 4
    for m0 in range(0, tm, sub):
        rows = slice(m0, m0 + sub)
        c, sa, sb = [jnp.concatenate([t_ref[rows, :] * r[rows]] * rep, axis=1)
                     for t_ref in (c_ref, sa_ref, sb_ref)]
        acc = _dot(x[rows], w)
        lo = pltpu.roll(acc, half, axis=1)
        hi = pltpu.roll(acc, tn - half, axis=1)
        o_ref[rows, :] = (acc * c + lo * sa + hi * sb).astype(o_ref.dtype)


def _mm_rope(normed, gain, w, layer, tables, col_block_of, n_blocks, blocks_per_part, q_scale,
             tm=1024, tn=512):
    m, k = normed[0].shape
    c, sa, sb = tables
    tab_spec = pl.BlockSpec((tm, HEAD_DIM), lambda i, j: (i, 0))
    return pl.pallas_call(
        functools.partial(_mm_rope_body, q_scale=q_scale, blocks_per_part=blocks_per_part),
        grid=(m // tm, n_blocks),
        in_specs=_normed_specs(tm, k) + [_layer_cols(layer, k, tn, col_block_of),
                                         tab_spec, tab_spec, tab_spec],
        out_specs=pl.BlockSpec((tm, tn), lambda i, j: (i, j)),
        out_shape=jax.ShapeDtypeStruct((m, n_blocks * tn), BF16),
        compiler_params=_params("parallel", "arbitrary"),
        name="mm_rope",
    )(*normed, gain, w, c, sa, sb)


def _rope_tables(seq):
    half = ROT_DIM // 2
    inv = ROPE_THETA ** (-jnp.arange(0, ROT_DIM, 2, dtype=F32) / ROT_DIM)
    ang = jnp.arange(seq, dtype=F32)[:, None] * inv[None, :]
    cos, sin = jnp.cos(ang), jnp.sin(ang)
    ones = jnp.ones((seq, HEAD_DIM - ROT_DIM), F32)
    zeros = jnp.zeros((seq, HEAD_DIM - ROT_DIM), F32)
    zh = jnp.zeros((seq, half), F32)
    c = jnp.concatenate([cos, cos, ones], axis=1)
    sa = jnp.concatenate([zh, sin, zeros], axis=1)
    sb = jnp.concatenate([-sin, zh, zeros], axis=1)
    return c, sa, sb


def _diff_attn_body(lam_ref, q_ref, k_ref, v_ref, g_ref, o_ref,
                    s_scr, mpart_scr, m_scr, lpart_scr, acc_scr, *, tu, tk, lam_init):
    tq = q_ref.shape[0]
    nk = k_ref.shape[0] // tk
    nb = tk // LANES
    nr = tq // tu

    def rows(r):
        if isinstance(r, int):
            return slice(r * tu, (r + 1) * tu)
        return pl.ds(pl.multiple_of(r * tu, tu), tu)

    def scores_chunk(r, c, ch):
        q = q_ref[rows(r), c * HEAD_DIM:(c + 1) * HEAD_DIM]
        k = k_ref[pl.ds(pl.multiple_of(ch * tk, tk), tk), c * HEAD_DIM:(c + 1) * HEAD_DIM]
        s = _dot_nt(q, k)
        s_scr[c, ch] = s
        mp = s[:, :LANES]
        for b in range(1, nb):
            mp = jnp.maximum(mp, s[:, b * LANES:(b + 1) * LANES])
        mpart_scr[c] = jnp.maximum(mpart_scr[c], mp)

    def scores_begin(c):
        mpart_scr[c] = jnp.full((tu, LANES), -jnp.inf, F32)

    def scores_end(c):
        m = jnp.max(mpart_scr[c], axis=1, keepdims=True)
        m_scr[c] = jnp.broadcast_to(m, (tu, LANES))

    def values_chunk(c, ch):
        p = jnp.exp2(s_scr[c, ch] - jnp.concatenate([m_scr[c]] * nb, axis=1))
        lp = p[:, :LANES]
        for b in range(1, nb):
            lp = lp + p[:, b * LANES:(b + 1) * LANES]
        lpart_scr[...] += lp
        acc_scr[c] += _dot(p.astype(BF16), v_ref[pl.ds(pl.multiple_of(ch * tk, tk), tk), :])

    def values_begin(c):
        lpart_scr[...] = jnp.zeros((tu, LANES), F32)
        acc_scr[c] = jnp.zeros((tu, 2 * HEAD_DIM), F32)

    def values_end(r, c):
        l = jnp.sum(lpart_scr[...], axis=1, keepdims=True)
        acc_scr[c] = acc_scr[c] / l
        if c == 1:
            lv = lam_ref[...]
            lam = (jnp.exp(jnp.sum(lv[0:1] * lv[1:2], axis=1, keepdims=True))
                   - jnp.exp(jnp.sum(lv[2:3] * lv[3:4], axis=1, keepdims=True)) + lam_init)
            o = acc_scr[0] - lam * acc_scr[1]
            ms = jnp.mean(o * o, axis=-1, keepdims=True)
            o = o * lax.rsqrt(ms + NORM_EPS) * g_ref[...]
            o_ref[rows(r), :] = (o * (1.0 - lam_init)).astype(o_ref.dtype)

    def phase(r_scores, c_scores, r_values, c_values):
        if c_scores is not None:
            scores_begin(c_scores)
        if c_values is not None:
            values_begin(c_values)

        def chunk(ch, carry):
            if c_scores is not None:
                scores_chunk(r_scores, c_scores, ch)
            if c_values is not None:
                values_chunk(c_values, ch)
            return carry

        lax.fori_loop(0, nk, chunk, 0, unroll=min(nk, 4))
        if c_scores is not None:
            scores_end(c_scores)
        if c_values is not None:
            values_end(r_values, c_values)

    phase(0, 0, None, None)

    def row(r, carry):
        phase(r, 1, r, 0)
        phase(r + 1, 0, r, 1)
        return carry

    lax.fori_loop(0, nr - 1, row, 0)
    phase(nr - 1, 1, nr - 1, 0)
    phase(None, None, nr - 1, 1)


def _diff_attn(qk, v, lam_vecs, subln_g, lam_init, q_col0, k_col0, v_col0, tq=4096, tu=256, tk=1024):
    s = qk.shape[0]
    hw = 2 * HEAD_DIM
    return pl.pallas_call(
        functools.partial(_diff_attn_body, tu=tu, tk=tk, lam_init=lam_init),
        grid=(DIFF_HEADS, s // tq),
        in_specs=[pl.BlockSpec((4, HEAD_DIM), lambda h, i: (0, 0)),
                  pl.BlockSpec((tq, hw), lambda h, i: (i, q_col0 + h)),
                  pl.BlockSpec((s, hw), lambda h, i: (0, k_col0 + h)),
                  pl.BlockSpec((s, hw), lambda h, i: (0, v_col0 + h)),
                  pl.BlockSpec((1, hw), lambda h, i: (0, 0))],
        out_specs=pl.BlockSpec((tq, hw), lambda h, i: (i, h)),
        out_shape=jax.ShapeDtypeStruct((s, DIFF_HEADS * hw), BF16),
        scratch_shapes=[pltpu.VMEM((2, s // tk, tu, tk), F32),
                        pltpu.VMEM((2, tu, LANES), F32),
                        pltpu.VMEM((2, tu, LANES), F32),
                        pltpu.VMEM((tu, LANES), F32),
                        pltpu.VMEM((2, tu, hw), F32)],
        compiler_params=_params("parallel", "arbitrary"),
        name="diff_attn",
    )(lam_vecs, qk, qk, v, subln_g.reshape(1, hw))


def _dil_window(i, t, dil, seq):
    reach = DIL_RADIUS * dil
    width = t + 2 * reach
    start = jnp.clip(i * t - reach, 0, seq - width)
    return start, width


def _dil_attn_body(*refs, t, seq, casts):
    n = len(DIL_PATTERNS)
    n_in, n_out = len(casts.in_specs), len(casts.out_specs)
    q_refs, k_refs, v_refs = refs[:n], refs[n:2 * n], refs[2 * n:3 * n]
    o_ref = refs[3 * n + n_in]
    bias_refs = refs[3 * n + n_in + 1 + n_out:]
    casts.run(refs[3 * n:3 * n + n_in], refs[3 * n + n_in + 1:3 * n + n_in + 1 + n_out])
    i = pl.program_id(1)
    outs, lses = [], []
    for g, (_, dil) in enumerate(DIL_PATTERNS):
        start, width = _dil_window(i, t, dil, seq)
        start = pl.multiple_of(start, DIL_RADIUS)
        offset = start - i * t
        prev_start, _ = _dil_window(i - 1, t, dil, seq)
        bias_ref = bias_refs[g]

        @pl.when((i == 0) | (offset != prev_start - (i - 1) * t))
        def _(bias_ref=bias_ref, offset=offset, width=width, dil=dil):
            rel = (offset + lax.broadcasted_iota(jnp.int32, (t, width), 1)
                   - lax.broadcasted_iota(jnp.int32, (t, width), 0))
            off_band = jnp.where(jnp.abs(rel) <= DIL_RADIUS * dil, rel & (dil - 1), 1)
            bias_ref[...] = jnp.where(off_band == 0, 0.0, MASK_VALUE)

        kw = k_refs[g][pl.ds(start, width), :]
        vw = v_refs[g][pl.ds(start, width), :]
        sc = _dot_nt(q_refs[g][...], kw) + bias_ref[...]
        m = jnp.max(sc, axis=-1, keepdims=True)
        p = jnp.exp2(sc - m)
        den = jnp.sum(p, axis=-1, keepdims=True)
        outs.append(_dot(p.astype(BF16), vw) / den)
        lses.append(m + jnp.log2(den))
    top = functools.reduce(jnp.maximum, lses)
    ws = [jnp.exp2(lse - top) for lse in lses]
    total = functools.reduce(lambda a, b: a + b, ws)
    out = functools.reduce(lambda a, b: a + b, [(w / total) * o for w, o in zip(ws, outs)])
    o_ref[...] = out.astype(o_ref.dtype)


def _dil_attn(qk, v, q_col0, k_col0, v_col0, cast_jobs, t=256):
    s = qk.shape[0]
    hg = DIL_HEADS_PER_GROUP
    n = len(DIL_PATTERNS)
    casts = _SlabCasts(cast_jobs, hg, s // t)

    def block(col0, g):
        return pl.BlockSpec((t, HEAD_DIM), lambda hh, i: (i, col0 + hg * g + hh))

    def full(col0, g):
        return pl.BlockSpec((s, HEAD_DIM), lambda hh, i: (0, col0 + hg * g + hh))

    widths = [t + 2 * DIL_RADIUS * dil for _, dil in DIL_PATTERNS]
    assert all(window // (2 * dil) == DIL_RADIUS for window, dil in DIL_PATTERNS)
    outs = pl.pallas_call(
        functools.partial(_dil_attn_body, t=t, seq=s, casts=casts),
        grid=(hg, s // t),
        in_specs=([block(q_col0, g) for g in range(n)] + [full(k_col0, g) for g in range(n)]
                  + [full(v_col0, g) for g in range(n)] + casts.in_specs),
        out_specs=[pl.BlockSpec((t, HEAD_DIM), lambda hh, i: (i, hh))] + casts.out_specs,
        out_shape=[jax.ShapeDtypeStruct((s, hg * HEAD_DIM), BF16)] + casts.out_shapes,
        scratch_shapes=[pltpu.VMEM((t, w), F32) for w in widths],
        compiler_params=_params("arbitrary", "arbitrary"),
        name="dil_attn",
    )(*([qk] * n), *([qk] * n), *([v] * n), *casts.operands)
    return outs[0], outs[1:]


def _row_mask(i, tm, seq):
    row = i * tm - HALO + lax.broadcasted_iota(jnp.int32, (tm + 2 * HALO, 1), 0)
    return jnp.where(row >= 0, jnp.where(row < seq, 1.0, 0.0), 0.0)


def _shift_rows(x, k):
    n = x.shape[0]
    return pltpu.roll(x, (-k) % n, axis=0)


def _conv_body(b_ref, cp_ref, cm_ref, cn_ref, up_ref, um_ref, un_ref, w_ref, o_ref, *, seq):
    tm = b_ref.shape[0]
    i = pl.program_id(0)
    c_ext = jnp.concatenate([cp_ref[...], cm_ref[...], cn_ref[...]], axis=0).astype(F32)
    u_ext = jnp.concatenate([up_ref[...], um_ref[...], un_ref[...]], axis=0).astype(F32)
    z = c_ext * u_ext * _row_mask(i, tm, seq)
    w = w_ref[...]
    y = w[0:1] * _shift_rows(z, -1) + w[1:2] * z + w[2:3] * _shift_rows(z, 1)
    y = y[HALO:HALO + tm]
    o_ref[...] = (b_ref[...].astype(F32) * y).astype(o_ref.dtype)


def _conv_mixer(rest, w_conv, layer, b_col, c_col, u_col, width, tm=512):
    s = rest.shape[0]
    hb = tm // HALO
    n_halo = s // HALO

    def main(col):
        return pl.BlockSpec((tm, width), lambda i: (i, col))

    def prev(col):
        return pl.BlockSpec((HALO, width), lambda i: (jnp.maximum(i * hb - 1, 0), col))

    def nxt(col):
        return pl.BlockSpec((HALO, width), lambda i: (jnp.minimum((i + 1) * hb, n_halo - 1), col))

    return pl.pallas_call(
        functools.partial(_conv_body, seq=s),
        grid=(s // tm,),
        in_specs=[main(b_col), prev(c_col), main(c_col), nxt(c_col),
                  prev(u_col), main(u_col), nxt(u_col),
                  pl.BlockSpec((None,) + w_conv.shape[1:], lambda i: (layer, 0, 0))],
        out_specs=pl.BlockSpec((tm, width), lambda i: (i, 0)),
        out_shape=jax.ShapeDtypeStruct((s, width), BF16),
        compiler_params=_params("parallel"),
        name="conv_mixer",
    )(rest, rest, rest, rest, rest, rest, rest, w_conv)


def _pool_body(up_ref, um_ref, un_ref, w_ref, sc_ref, o_ref, *, seq):
    tm = um_ref.shape[0]
    i = pl.program_id(0)
    ext = jnp.concatenate([up_ref[...], um_ref[...], un_ref[...]], axis=0).astype(F32)
    ext = ext * _row_mask(i, tm, seq)
    pos = i * tm + lax.broadcasted_iota(jnp.int32, (tm, 1), 0)
    gw = ext.shape[1] // len(POOL_WINDOWS)
    for gi, win in enumerate(POOL_WINDOWS):
        sl = slice(gi * gw, (gi + 1) * gw)
        xg = ext[:, sl]
        acc = xg
        span = 1
        while span < win:
            acc = acc + _shift_rows(acc, span)
            span *= 2
        total = _shift_rows(acc, -(win // 2))[HALO:HALO + tm]
        lo = jnp.clip(pos - win // 2, 0, seq)
        hi = jnp.clip(pos - win // 2 + win, 0, seq)
        cnt = (hi - lo).astype(F32)
        pooled = total / cnt - xg[HALO:HALO + tm]
        y = _dot(pooled.astype(BF16), w_ref[gi].astype(BF16))
        o_ref[:, sl] = (y * sc_ref[:, sl]).astype(o_ref.dtype)


def _pool_mixer(rest, w_pool, pool_scale, layer, u_col, width, tm=512):
    s = rest.shape[0]
    hb = tm // HALO
    n_halo = s // HALO
    return pl.pallas_call(
        functools.partial(_pool_body, seq=s),
        grid=(s // tm,),
        in_specs=[pl.BlockSpec((HALO, width), lambda i: (jnp.maximum(i * hb - 1, 0), u_col)),
                  pl.BlockSpec((tm, width), lambda i: (i, u_col)),
                  pl.BlockSpec((HALO, width), lambda i: (jnp.minimum((i + 1) * hb, n_halo - 1), u_col)),
                  pl.BlockSpec((None,) + w_pool.shape[1:], lambda i: (layer, 0, 0, 0)),
                  pl.BlockSpec((1, width), lambda i: (0, 0))],
        out_specs=pl.BlockSpec((tm, width), lambda i: (i, 0)),
        out_shape=jax.ShapeDtypeStruct((s, width), BF16),
        compiler_params=_params("parallel"),
        name="pool_mixer",
    )(rest, rest, rest, w_pool, pool_scale.reshape(1, width))


def _merge_body(ya_ref, yb_ref, yc_ref, yd_ref, gl_ref, wa_ref, wb_ref, wc_ref, wd_ref,
                g0_ref, g1_ref, g2_ref, g3_ref, b0_ref, b1_ref, b2_ref, b3_ref, *rest, casts):
    n_in = len(casts.in_specs)
    o_ref = rest[n_in]
    casts.run(rest[:n_in], rest[n_in + 1:])
    gl = gl_ref[...]
    branches = ((ya_ref, wa_ref, g0_ref, b0_ref), (yb_ref, wb_ref, g1_ref, b1_ref),
                (yc_ref, wc_ref, g2_ref, b2_ref), (yd_ref, wd_ref, g3_ref, b3_ref))
    merged = None
    for y_ref, w_ref, g_ref, b_ref in branches:
        gate = jax.nn.sigmoid(_dot(gl, g_ref[...].astype(BF16)) + b_ref[...])
        term = gate * _dot(y_ref[...], w_ref[...].astype(BF16))
        merged = term if merged is None else merged + term
    o_ref[...] = merged.astype(o_ref.dtype)


def _merge(ya, yb, yc, yd, rest, gl_col, w_a, w_b, w_c, w_d, w_gate_up, b_gate, cast_jobs,
           layer, tm=1024, tn=256):
    s = ya.shape[0]
    d = w_a.shape[2]
    rank = w_gate_up.shape[1]
    nj = d // tn

    def act(arr):
        return pl.BlockSpec((tm, arr.shape[1]), lambda i, j: (i, 0))

    def wgt(arr):
        return _layer_cols(layer, arr.shape[1], tn, lambda j: j)

    def gate_w(b):
        return _layer_cols(layer, rank, tn, lambda j: b * nj + j)

    def gate_b(b):
        return _layer_cols(layer, 1, tn, lambda j: b * nj + j)

    bias = b_gate.reshape(b_gate.shape[0], 1, N_BRANCHES * d)
    casts = _SlabCasts(cast_jobs, s // tm, nj)
    outs = pl.pallas_call(
        functools.partial(_merge_body, casts=casts),
        grid=(s // tm, nj),
        in_specs=[act(ya), act(yb), act(yc), act(yd),
                  pl.BlockSpec((tm, rank), lambda i, j: (i, gl_col)),
                  wgt(w_a), wgt(w_b), wgt(w_c), wgt(w_d),
                  gate_w(0), gate_w(1), gate_w(2), gate_w(3),
                  gate_b(0), gate_b(1), gate_b(2), gate_b(3)] + casts.in_specs,
        out_specs=[pl.BlockSpec((tm, tn), lambda i, j: (i, j))] + casts.out_specs,
        out_shape=[jax.ShapeDtypeStruct((s, d), BF16)] + casts.out_shapes,
        compiler_params=_params("parallel", "arbitrary"),
        name="merge",
    )(ya, yb, yc, yd, rest, w_a, w_b, w_c, w_d,
      w_gate_up, w_gate_up, w_gate_up, w_gate_up, bias, bias, bias, bias, *casts.operands)
    return outs[0], outs[1:]


def kernel(x, ln_ffn1, w1_gate, w1_up, w1_down, ln_mix, w_in, lambda_q1, lambda_k1, lambda_q2, lambda_k2, subln, conv_w, pool_w, pool_scale, w_gate_up, b_gate, w_branch_a, w_branch_b, w_branch_c, w_branch_d, w_out, ln_ffn2, w2_gate, w2_up, w2_down, ln_final):
    b, s, d = x.shape
    depth = ln_ffn1.shape[0]
    mix = 3 * d // 8
    tables = _rope_tables(s)

    tn_rope, tn_cols = 512, 512
    pr, pc = mix // tn_rope, mix // tn_cols
    n_rest = (w_in.shape[2] - 4 * mix) // tn_cols
    hd_per_mix = mix // HEAD_DIM
    q_scale = HEAD_DIM ** -0.5 * math.log2(math.e)

    outs = []
    for bi in range(b):
        xs = x[bi]
        normed = _prep(xs)
        ffn1_w = (_cast_layer(w1_gate, 0, ln_ffn1[0]), _cast_layer(w1_up, 0, ln_ffn1[0]))
        for l in range(depth):
            a, (w_down,) = _ffn_up(normed, *ffn1_w, [(w1_down, l, None)])
            xs, normed = _mm_residual(a, w_down, xs, 0.5)

            gain = _lane_replicated(ln_mix[l])
            qk = _mm_rope(normed, gain, w_in, l, tables,
                          lambda j: jnp.where(j < 2 * pr, j, j + pr), 4 * pr, pr, q_scale,
                          tn=tn_rope)
            rest = _mm_cols(normed, gain, w_in, l,
                            lambda j: jnp.where(j < pc, j + 2 * pc, j + 4 * pc), n_rest, tn=tn_cols)

            lam_init = 0.8 - 0.6 * math.exp(-0.3 * l)
            lam_vecs = jnp.stack([lambda_q1[l], lambda_k1[l], lambda_q2[l], lambda_k2[l]])
            ya = _diff_attn(qk, rest, lam_vecs, subln[l], lam_init,
                            q_col0=0, k_col0=DIFF_HEADS, v_col0=0)

            ahead = ([(w1_gate, l + 1, ln_ffn1[l + 1]), (w1_up, l + 1, ln_ffn1[l + 1])]
                     if l + 1 < depth else [])
            yb, ffn1_w = _dil_attn(qk, rest, q_col0=2 * hd_per_mix, k_col0=3 * hd_per_mix,
                                   v_col0=hd_per_mix, cast_jobs=ahead)

            yc = _conv_mixer(rest, conv_w, l, b_col=2, c_col=3, u_col=4, width=mix)
            yd = _pool_mixer(rest, pool_w, pool_scale[l], l, u_col=5, width=mix)

            rank = w_gate_up.shape[1]
            merged, (w_out_l, *ffn2_w) = _merge(
                ya, yb, yc, yd, rest, 6 * mix // rank, w_branch_a, w_branch_b, w_branch_c,
                w_branch_d, w_gate_up, b_gate,
                [(w_out, l, None), (w2_gate, l, ln_ffn2[l]), (w2_up, l, ln_ffn2[l])], l)
            xs, normed = _mm_residual(merged, w_out_l, xs, 1.0)

            a, (w_down,) = _ffn_up(normed, *ffn2_w, [(w2_down, l, None)])
            xs, normed = _mm_residual(a, w_down, xs, 0.5)
        outs.append(_rmsnorm(xs, ln_final, F32))
    return jnp.stack(outs, axis=0)
```

```python
import functools
import math

import jax
import jax.numpy as jnp
from jax import lax
from jax.experimental import pallas as pl
from jax.experimental.pallas import tpu as pltpu

F32 = jnp.float32
BF16 = jnp.bfloat16

LANES = 128
HEAD_DIM = 128
ROT_DIM = HEAD_DIM // 4
ROPE_THETA = 500000.0
NORM_EPS = 1e-5
DIFF_HEADS = 6
DIL_PATTERNS = ((128, 1), (512, 4), (2048, 16))
DIL_HEADS_PER_GROUP = 4
DIL_RADIUS = 64
POOL_WINDOWS = (2, 4, 8, 16)
N_BRANCHES = 4
HALO = 16
MASK_VALUE = -1e30

VMEM_LIMIT_BYTES = 56 * 1024 * 1024


def _params(*sem):
    return pltpu.CompilerParams(dimension_semantics=sem, vmem_limit_bytes=VMEM_LIMIT_BYTES)


def _dot(a, b):
    return jnp.dot(a, b, preferred_element_type=F32)


def _dot_nt(a, b):
    return lax.dot_general(a, b, (((1,), (1,)), ((), ())), preferred_element_type=F32)


def _layer_cols(layer, k, tn, col_of):
    return pl.BlockSpec((None, k, tn), lambda i, j: (layer, 0, col_of(j)))


def _rmsnorm_body(x_ref, g_ref, o_ref):
    x = x_ref[...]
    ms = jnp.mean(x * x, axis=-1, keepdims=True)
    o_ref[...] = (x * lax.rsqrt(ms + NORM_EPS) * g_ref[...]).astype(o_ref.dtype)


def _rmsnorm(x, g, out_dtype, tm=256):
    s, d = x.shape
    return pl.pallas_call(
        _rmsnorm_body,
        grid=(s // tm,),
        in_specs=[pl.BlockSpec((tm, d), lambda i: (i, 0)),
                  pl.BlockSpec((1, d), lambda i: (0, 0))],
        out_specs=pl.BlockSpec((tm, d), lambda i: (i, 0)),
        out_shape=jax.ShapeDtypeStruct((s, d), out_dtype),
        compiler_params=_params("parallel"),
        name="rmsnorm",
    )(x, g.reshape(1, d))


def _lane_partial_sumsq(y):
    acc = None
    for blk in range(y.shape[1] // LANES):
        part = y[:, blk * LANES:(blk + 1) * LANES]
        acc = part * part if acc is None else acc + part * part
    return acc


def _row_rsqrt(ss_ref, width):
    return lax.rsqrt(jnp.sum(ss_ref[...], axis=1, keepdims=True) / width + NORM_EPS)


def _gained_bf16(w_ref, g_ref):
    w = w_ref[...]
    return (w * jnp.concatenate([g_ref[...]] * (w.shape[1] // LANES), axis=1)).astype(BF16)


def _lane_replicated(g):
    return jnp.broadcast_to(g[:, None], (g.shape[0], LANES))


def _prep_body(x_ref, xb_ref, ss_ref):
    x = x_ref[...]
    xb_ref[...] = x.astype(BF16)
    ss_ref[...] = _lane_partial_sumsq(x)


def _prep(x, tm=256):
    s, d = x.shape
    return pl.pallas_call(
        _prep_body,
        grid=(s // tm,),
        in_specs=[pl.BlockSpec((tm, d), lambda i: (i, 0))],
        out_specs=[pl.BlockSpec((tm, d), lambda i: (i, 0)),
                   pl.BlockSpec((tm, LANES), lambda i: (i, 0))],
        out_shape=[jax.ShapeDtypeStruct((s, d), BF16),
                   jax.ShapeDtypeStruct((s, LANES), F32)],
        compiler_params=_params("parallel"),
        name="prep",
    )(x)


def _normed_specs(tm, k):
    return [pl.BlockSpec((tm, k), lambda i, j: (i, 0)),
            pl.BlockSpec((tm, LANES), lambda i, j: (i, 0)),
            pl.BlockSpec((k, LANES), lambda i, j: (0, 0))]


class _SlabCasts:
    def __init__(self, jobs, n_i, n_j):
        self.in_specs, self.out_specs, self.out_shapes, self.operands, self.has_gain = [], [], [], [], []
        for w, layer, gain in jobs:
            rows, cols = w.shape[1:]
            slab = rows // (n_i * n_j)
            assert slab * n_i * n_j == rows and slab % 16 == 0, (rows, n_i, n_j)
            self.in_specs.append(pl.BlockSpec((None, slab, cols),
                                              lambda i, j, layer=layer: (layer, i * n_j + j, 0)))
            self.operands.append(w)
            if gain is not None:
                self.in_specs.append(pl.BlockSpec((slab, LANES), lambda i, j: (i * n_j + j, 0)))
                self.operands.append(_lane_replicated(gain))
            self.has_gain.append(gain is not None)
            self.out_specs.append(pl.BlockSpec((slab, cols), lambda i, j: (i * n_j + j, 0)))
            self.out_shapes.append(jax.ShapeDtypeStruct((rows, cols), BF16))

    def run(self, in_refs, out_refs):
        in_refs = list(in_refs)
        for has_gain, dst in zip(self.has_gain, out_refs):
            w = in_refs.pop(0)[...]
            if has_gain:
                w = w * jnp.concatenate([in_refs.pop(0)[...]] * (w.shape[1] // LANES), axis=1)
            dst[...] = w.astype(BF16)


def _cast_layer(w, layer, gain, n_steps=16):
    casts = _SlabCasts([(w, layer, gain)], n_steps, 1)

    def body(*refs):
        casts.run(refs[:-1], refs[-1:])

    return pl.pallas_call(
        body, grid=(n_steps, 1), in_specs=casts.in_specs, out_specs=casts.out_specs[0],
        out_shape=casts.out_shapes[0], compiler_params=_params("parallel", "arbitrary"),
        name="cast_layer",
    )(*casts.operands)


def _ffn_up_body(*refs, casts):
    n_in, n_out = len(casts.in_specs), len(casts.out_specs)
    x_ref, ss_ref, wg_ref, wu_ref = refs[:4]
    o_ref = refs[4 + n_in]
    x = x_ref[...]
    r = _row_rsqrt(ss_ref, x.shape[1])
    g = r * _dot(x, wg_ref[...])
    u = r * _dot(x, wu_ref[...])
    o_ref[...] = (g * jax.nn.sigmoid(g) * u).astype(o_ref.dtype)
    casts.run(refs[4:4 + n_in], refs[5 + n_in:5 + n_in + n_out])


def _ffn_up(normed, wg, wu, cast_jobs, tm=1024, tn=512):
    m, k = normed[0].shape
    n = wg.shape[1]
    n_i, n_j = m // tm, n // tn
    casts = _SlabCasts(cast_jobs, n_i, n_j)
    outs = pl.pallas_call(
        functools.partial(_ffn_up_body, casts=casts),
        grid=(n_i, n_j),
        in_specs=[pl.BlockSpec((tm, k), lambda i, j: (i, 0)),
                  pl.BlockSpec((tm, LANES), lambda i, j: (i, 0)),
                  pl.BlockSpec((k, tn), lambda i, j: (0, j)),
                  pl.BlockSpec((k, tn), lambda i, j: (0, j))] + casts.in_specs,
        out_specs=[pl.BlockSpec((tm, tn), lambda i, j: (i, j))] + casts.out_specs,
        out_shape=[jax.ShapeDtypeStruct((m, n), BF16)] + casts.out_shapes,
        compiler_params=_params("parallel", "arbitrary"),
        name="ffn_up",
    )(*normed, wg, wu, *casts.operands)
    return outs[0], outs[1:]


def _mm_residual_body(a_ref, w_ref, r_ref, o_ref, ob_ref, ss_ref, *, alpha):
    y = r_ref[...] + alpha * _dot(a_ref[...], w_ref[...])
    o_ref[...] = y
    ob_ref[...] = y.astype(BF16)
    part = _lane_partial_sumsq(y)
    j = pl.program_id(1)

    @pl.when(j == 0)
    def _():
        ss_ref[...] = part

    @pl.when(j > 0)
    def _():
        ss_ref[...] += part


def _mm_residual(a, w, res, alpha, tm=1024, tn=512):
    m, k = a.shape
    n = w.shape[1]
    y, yb, ss = pl.pallas_call(
        functools.partial(_mm_residual_body, alpha=alpha),
        grid=(m // tm, n // tn),
        in_specs=[pl.BlockSpec((tm, k), lambda i, j: (i, 0)),
                  pl.BlockSpec((k, tn), lambda i, j: (0, j)),
                  pl.BlockSpec((tm, tn), lambda i, j: (i, j))],
        out_specs=[pl.BlockSpec((tm, tn), lambda i, j: (i, j)),
                   pl.BlockSpec((tm, tn), lambda i, j: (i, j)),
                   pl.BlockSpec((tm, LANES), lambda i, j: (i, 0))],
        out_shape=[jax.ShapeDtypeStruct((m, n), F32),
                   jax.ShapeDtypeStruct((m, n), BF16),
                   jax.ShapeDtypeStruct((m, LANES), F32)],
        compiler_params=_params("parallel", "arbitrary"),
        name="mm_residual",
    )(a, w, res)
    return y, (yb, ss)


def _mm_cols_body(x_ref, ss_ref, g_ref, w_ref, o_ref):
    x = x_ref[...]
    acc = _row_rsqrt(ss_ref, x.shape[1]) * _dot(x, _gained_bf16(w_ref, g_ref))
    o_ref[...] = acc.astype(o_ref.dtype)


def _mm_cols(normed, gain, w, layer, col_block_of, n_blocks, tm=1024, tn=512):
    m, k = normed[0].shape
    return pl.pallas_call(
        _mm_cols_body,
        grid=(m // tm, n_blocks),
        in_specs=_normed_specs(tm, k) + [_layer_cols(layer, k, tn, col_block_of)],
        out_specs=pl.BlockSpec((tm, tn), lambda i, j: (i, j)),
        out_shape=jax.ShapeDtypeStruct((m, n_blocks * tn), BF16),
        compiler_params=_params("parallel", "arbitrary"),
        name="mm_cols",
    )(*normed, gain, w)


def _mm_rope_body(x_ref, ss_ref, g_ref, w_ref, c_ref, sa_ref, sb_ref, o_ref, *, q_scale,
                  blocks_per_part):
    x = x_ref[...]
    j = pl.program_id(1)
    is_q = (j // blocks_per_part) % 2 == 0
    r = _row_rsqrt(ss_ref, x.shape[1]) * jnp.where(is_q, q_scale, 1.0)
    half = ROT_DIM // 2
    tm, tn = o_ref.shape
    rep = tn // HEAD_DIM
    w = _gained_bf16(w_ref, g_ref)
    sub = tm // 4
    for m0 in range(0, tm, sub):
        rows = slice(m0, m0 + sub)
        c, sa, sb = [jnp.concatenate([t_ref[rows, :] * r[rows]] * rep, axis=1)
                     for t_ref in (c_ref, sa_ref, sb_ref)]
        acc = _dot(x[rows], w)
        lo = pltpu.roll(acc, half, axis=1)
        hi = pltpu.roll(acc, tn - half, axis=1)
        o_ref[rows, :] = (acc * c + lo * sa + hi * sb).astype(o_ref.dtype)


def _mm_rope(normed, gain, w, layer, tables, col_block_of, n_blocks, blocks_per_part, q_scale,
             tm=1024, tn=512):
    m, k = normed[0].shape
    c, sa, sb = tables
    tab_spec = pl.BlockSpec((tm, HEAD_DIM), lambda i, j: (i, 0))
    return pl.pallas_call(
        functools.partial(_mm_rope_body, q_scale=q_scale, blocks_per_part=blocks_per_part),
        grid=(m // tm, n_blocks),
        in_specs=_normed_specs(tm, k) + [_layer_cols(layer, k, tn, col_block_of),
                                         tab_spec, tab_spec, tab_spec],
        out_specs=pl.BlockSpec((tm, tn), lambda i, j: (i, j)),
        out_shape=jax.ShapeDtypeStruct((m, n_blocks * tn), BF16),
        compiler_params=_params("parallel", "arbitrary"),
        name="mm_rope",
    )(*normed, gain, w, c, sa, sb)


def _rope_tables(seq):
    half = ROT_DIM // 2
    inv = ROPE_THETA ** (-jnp.arange(0, ROT_DIM, 2, dtype=F32) / ROT_DIM)
    ang = jnp.arange(seq, dtype=F32)[:, None] * inv[None, :]
    cos, sin = jnp.cos(ang), jnp.sin(ang)
    ones = jnp.ones((seq, HEAD_DIM - ROT_DIM), F32)
    zeros = jnp.zeros((seq, HEAD_DIM - ROT_DIM), F32)
    zh = jnp.zeros((seq, half), F32)
    c = jnp.concatenate([cos, cos, ones], axis=1)
    sa = jnp.concatenate([zh, sin, zeros], axis=1)
    sb = jnp.concatenate([-sin, zh, zeros], axis=1)
    return c, sa, sb


def _diff_attn_body(lam_ref, q_ref, k_ref, v_ref, g_ref, o_ref,
                    s_scr, mpart_scr, m_scr, lpart_scr, acc_scr, *, tu, tk, lam_init):
    tq = q_ref.shape[0]
    nk = k_ref.shape[0] // tk
    nb = tk // LANES
    nr = tq // tu

    def rows(r):
        if isinstance(r, int):
            return slice(r * tu, (r + 1) * tu)
        return pl.ds(pl.multiple_of(r * tu, tu), tu)

    def scores_chunk(r, c, ch):
        q = q_ref[rows(r), c * HEAD_DIM:(c + 1) * HEAD_DIM]
        k = k_ref[pl.ds(pl.multiple_of(ch * tk, tk), tk), c * HEAD_DIM:(c + 1) * HEAD_DIM]
        s = _dot_nt(q, k)
        s_scr[c, ch, :, c * LANES:c * LANES + tk] = s
        mp = s[:, :LANES]
        for b in range(1, nb):
            mp = jnp.maximum(mp, s[:, b * LANES:(b + 1) * LANES])
        mpart_scr[c] = jnp.maximum(mpart_scr[c], mp)

    def scores_begin(c):
        mpart_scr[c] = jnp.full((tu, LANES), -jnp.inf, F32)

    def scores_end(c):
        m = jnp.max(mpart_scr[c], axis=1, keepdims=True)
        m_scr[c] = jnp.broadcast_to(m, (tu, LANES))

    def values_chunk(c, ch):
        s = s_scr[c, ch, :, c * LANES:c * LANES + tk]
        p = jnp.exp2(s - jnp.concatenate([m_scr[c]] * nb, axis=1))
        lp = p[:, :LANES]
        for b in range(1, nb):
            lp = lp + p[:, b * LANES:(b + 1) * LANES]
        lpart_scr[...] += lp
        acc_scr[c] += _dot(p.astype(BF16), v_ref[pl.ds(pl.multiple_of(ch * tk, tk), tk), :])

    def values_begin(c):
        lpart_scr[...] = jnp.zeros((tu, LANES), F32)
        acc_scr[c] = jnp.zeros((tu, 2 * HEAD_DIM), F32)

    def values_end(r, c):
        l = jnp.sum(lpart_scr[...], axis=1, keepdims=True)
        acc_scr[c] = acc_scr[c] / l
        if c == 1:
            lv = lam_ref[...]
            lam = (jnp.exp(jnp.sum(lv[0:1] * lv[1:2], axis=1, keepdims=True))
                   - jnp.exp(jnp.sum(lv[2:3] * lv[3:4], axis=1, keepdims=True)) + lam_init)
            o = acc_scr[0] - lam * acc_scr[1]
            ms = jnp.mean(o * o, axis=-1, keepdims=True)
            o = o * lax.rsqrt(ms + NORM_EPS) * g_ref[...]
            o_ref[rows(r), :] = (o * (1.0 - lam_init)).astype(o_ref.dtype)

    def phase(r_scores, c_scores, r_values, c_values):
        if c_scores is not None:
            scores_begin(c_scores)
        if c_values is not None:
            values_begin(c_values)

        def chunk(ch, carry):
            if c_scores is not None:
                scores_chunk(r_scores, c_scores, ch)
            if c_values is not None:
                values_chunk(c_values, ch)
            return carry

        lax.fori_loop(0, nk, chunk, 0, unroll=min(nk, 4))
        if c_scores is not None:
            scores_end(c_scores)
        if c_values is not None:
            values_end(r_values, c_values)

    phase(0, 0, None, None)

    def row(r, carry):
        phase(r, 1, r, 0)
        phase(r + 1, 0, r, 1)
        return carry

    lax.fori_loop(0, nr - 1, row, 0)
    phase(nr - 1, 1, nr - 1, 0)
    phase(None, None, nr - 1, 1)


def _diff_attn(qk, v, lam_vecs, subln_g, lam_init, q_col0, k_col0, v_col0, tq=4096, tu=256, tk=1024):
    s = qk.shape[0]
    hw = 2 * HEAD_DIM
    return pl.pallas_call(
        functools.partial(_diff_attn_body, tu=tu, tk=tk, lam_init=lam_init),
        grid=(DIFF_HEADS, s // tq),
        in_specs=[pl.BlockSpec((4, HEAD_DIM), lambda h, i: (0, 0)),
                  pl.BlockSpec((tq, hw), lambda h, i: (i, q_col0 + h)),
                  pl.BlockSpec((s, hw), lambda h, i: (0, k_col0 + h)),
                  pl.BlockSpec((s, hw), lambda h, i: (0, v_col0 + h)),
                  pl.BlockSpec((1, hw), lambda h, i: (0, 0))],
        out_specs=pl.BlockSpec((tq, hw), lambda h, i: (i, h)),
        out_shape=jax.ShapeDtypeStruct((s, DIFF_HEADS * hw), BF16),
        scratch_shapes=[pltpu.VMEM((2, s // tk, tu, tk + LANES), F32),
                        pltpu.VMEM((2, tu, LANES), F32),
                        pltpu.VMEM((2, tu, LANES), F32),
                        pltpu.VMEM((tu, LANES), F32),
                        pltpu.VMEM((2, tu, hw), F32)],
        compiler_params=_params("parallel", "arbitrary"),
        name="diff_attn",
    )(lam_vecs, qk, qk, v, subln_g.reshape(1, hw))


def _dil_window(i, t, dil, seq):
    reach = DIL_RADIUS * dil
    width = t + 2 * reach
    start = jnp.clip(i * t - reach, 0, seq - width)
    return start, width


def _dil_attn_body(*refs, t, seq, casts):
    n = len(DIL_PATTERNS)
    n_in, n_out = len(casts.in_specs), len(casts.out_specs)
    q_refs, k_refs, v_refs = refs[:n], refs[n:2 * n], refs[2 * n:3 * n]
    o_ref = refs[3 * n + n_in]
    bias_refs = refs[3 * n + n_in + 1 + n_out:]
    casts.run(refs[3 * n:3 * n + n_in], refs[3 * n + n_in + 1:3 * n + n_in + 1 + n_out])
    i = pl.program_id(1)
    outs, lses = [], []
    for g, (_, dil) in enumerate(DIL_PATTERNS):
        start, width = _dil_window(i, t, dil, seq)
        start = pl.multiple_of(start, DIL_RADIUS)
        offset = start - i * t
        prev_start, _ = _dil_window(i - 1, t, dil, seq)
        bias_ref = bias_refs[g]

        @pl.when((i == 0) | (offset != prev_start - (i - 1) * t))
        def _(bias_ref=bias_ref, offset=offset, width=width, dil=dil):
            rel = (offset + lax.broadcasted_iota(jnp.int32, (t, width), 1)
                   - lax.broadcasted_iota(jnp.int32, (t, width), 0))
            off_band = jnp.where(jnp.abs(rel) <= DIL_RADIUS * dil, rel & (dil - 1), 1)
            bias_ref[...] = jnp.where(off_band == 0, 0.0, MASK_VALUE)

        kw = k_refs[g][pl.ds(start, width), :]
        vw = v_refs[g][pl.ds(start, width), :]
        sc = _dot_nt(q_refs[g][...], kw) + bias_ref[...]
        m = jnp.max(sc, axis=-1, keepdims=True)
        p = jnp.exp2(sc - m)
        den = jnp.sum(p, axis=-1, keepdims=True)
        outs.append(_dot(p.astype(BF16), vw) / den)
        lses.append(m + jnp.log2(den))
    top = functools.reduce(jnp.maximum, lses)
    ws = [jnp.exp2(lse - top) for lse in lses]
    total = functools.reduce(lambda a, b: a + b, ws)
    out = functools.reduce(lambda a, b: a + b, [(w / total) * o for w, o in zip(ws, outs)])
    o_ref[...] = out.astype(o_ref.dtype)


def _dil_attn(qk, v, q_col0, k_col0, v_col0, cast_jobs, t=256):
    s = qk.shape[0]
    hg = DIL_HEADS_PER_GROUP
    n = len(DIL_PATTERNS)
    casts = _SlabCasts(cast_jobs, hg, s // t)

    def block(col0, g):
        return pl.BlockSpec((t, HEAD_DIM), lambda hh, i: (i, col0 + hg * g + hh))

    def full(col0, g):
        return pl.BlockSpec((s, HEAD_DIM), lambda hh, i: (0, col0 + hg * g + hh))

    widths = [t + 2 * DIL_RADIUS * dil for _, dil in DIL_PATTERNS]
    assert all(window // (2 * dil) == DIL_RADIUS for window, dil in DIL_PATTERNS)
    outs = pl.pallas_call(
        functools.partial(_dil_attn_body, t=t, seq=s, casts=casts),
        grid=(hg, s // t),
        in_specs=([block(q_col0, g) for g in range(n)] + [full(k_col0, g) for g in range(n)]
                  + [full(v_col0, g) for g in range(n)] + casts.in_specs),
        out_specs=[pl.BlockSpec((t, HEAD_DIM), lambda hh, i: (i, hh))] + casts.out_specs,
        out_shape=[jax.ShapeDtypeStruct((s, hg * HEAD_DIM), BF16)] + casts.out_shapes,
        scratch_shapes=[pltpu.VMEM((t, w), F32) for w in widths],
        compiler_params=_params("arbitrary", "arbitrary"),
        name="dil_attn",
    )(*([qk] * n), *([qk] * n), *([v] * n), *casts.operands)
    return outs[0], outs[1:]


def _row_mask(i, tm, seq):
    row = i * tm - HALO + lax.broadcasted_iota(jnp.int32, (tm + 2 * HALO, 1), 0)
    return jnp.where(row >= 0, jnp.where(row < seq, 1.0, 0.0), 0.0)


def _shift_rows(x, k):
    n = x.shape[0]
    return pltpu.roll(x, (-k) % n, axis=0)


def _conv_body(b_ref, cp_ref, cm_ref, cn_ref, up_ref, um_ref, un_ref, w_ref, o_ref, *, seq):
    tm = b_ref.shape[0]
    i = pl.program_id(0)
    c_ext = jnp.concatenate([cp_ref[...], cm_ref[...], cn_ref[...]], axis=0).astype(F32)
    u_ext = jnp.concatenate([up_ref[...], um_ref[...], un_ref[...]], axis=0).astype(F32)
    z = c_ext * u_ext * _row_mask(i, tm, seq)
    w = w_ref[...]
    y = w[0:1] * _shift_rows(z, -1) + w[1:2] * z + w[2:3] * _shift_rows(z, 1)
    y = y[HALO:HALO + tm]
    o_ref[...] = (b_ref[...].astype(F32) * y).astype(o_ref.dtype)


def _conv_mixer(rest, w_conv, layer, b_col, c_col, u_col, width, tm=512):
    s = rest.shape[0]
    hb = tm // HALO
    n_halo = s // HALO

    def main(col):
        return pl.BlockSpec((tm, width), lambda i: (i, col))

    def prev(col):
        return pl.BlockSpec((HALO, width), lambda i: (jnp.maximum(i * hb - 1, 0), col))

    def nxt(col):
        return pl.BlockSpec((HALO, width), lambda i: (jnp.minimum((i + 1) * hb, n_halo - 1), col))

    return pl.pallas_call(
        functools.partial(_conv_body, seq=s),
        grid=(s // tm,),
        in_specs=[main(b_col), prev(c_col), main(c_col), nxt(c_col),
                  prev(u_col), main(u_col), nxt(u_col),
                  pl.BlockSpec((None,) + w_conv.shape[1:], lambda i: (layer, 0, 0))],
        out_specs=pl.BlockSpec((tm, width), lambda i: (i, 0)),
        out_shape=jax.ShapeDtypeStruct((s, width), BF16),
        compiler_params=_params("parallel"),
        name="conv_mixer",
    )(rest, rest, rest, rest, rest, rest, rest, w_conv)


def _pool_body(up_ref, um_ref, un_ref, w_ref, sc_ref, o_ref, *, seq):
    tm = um_ref.shape[0]
    i = pl.program_id(0)
    ext = jnp.concatenate([up_ref[...], um_ref[...], un_ref[...]], axis=0).astype(F32)
    ext = ext * _row_mask(i, tm, seq)
    pos = i * tm + lax.broadcasted_iota(jnp.int32, (tm, 1), 0)
    gw = ext.shape[1] // len(POOL_WINDOWS)
    for gi, win in enumerate(POOL_WINDOWS):
        sl = slice(gi * gw, (gi + 1) * gw)
        xg = ext[:, sl]
        acc = xg
        span = 1
        while span < win:
            acc = acc + _shift_rows(acc, span)
            span *= 2
        total = _shift_rows(acc, -(win // 2))[HALO:HALO + tm]
        lo = jnp.clip(pos - win // 2, 0, seq)
        hi = jnp.clip(pos - win // 2 + win, 0, seq)
        cnt = (hi - lo).astype(F32)
        pooled = total / cnt - xg[HALO:HALO + tm]
        y = _dot(pooled.astype(BF16), w_ref[gi].astype(BF16))
        o_ref[:, sl] = (y * sc_ref[:, sl]).astype(o_ref.dtype)


def _pool_mixer(rest, w_pool, pool_scale, layer, u_col, width, tm=512):
    s = rest.shape[0]
    hb = tm // HALO
    n_halo = s // HALO
    return pl.pallas_call(
        functools.partial(_pool_body, seq=s),
        grid=(s // tm,),
        in_specs=[pl.BlockSpec((HALO, width), lambda i: (jnp.maximum(i * hb - 1, 0), u_col)),
                  pl.BlockSpec((tm, width), lambda i: (i, u_col)),
                  pl.BlockSpec((HALO, width), lambda i: (jnp.minimum((i + 1) * hb, n_halo - 1), u_col)),
                  pl.BlockSpec((None,) + w_pool.shape[1:], lambda i: (layer, 0, 0, 0)),
                  pl.BlockSpec((1, width), lambda i: (0, 0))],
        out_specs=pl.BlockSpec((tm, width), lambda i: (i, 0)),
        out_shape=jax.ShapeDtypeStruct((s, width), BF16),
        compiler_params=_params("parallel"),
        name="pool_mixer",
    )(rest, rest, rest, w_pool, pool_scale.reshape(1, width))


def _merge_body(ya_ref, yb_ref, yc_ref, yd_ref, gl_ref, wa_ref, wb_ref, wc_ref, wd_ref,
                g0_ref, g1_ref, g2_ref, g3_ref, b0_ref, b1_ref, b2_ref, b3_ref, *rest, casts):
    n_in = len(casts.in_specs)
    o_ref = rest[n_in]
    casts.run(rest[:n_in], rest[n_in + 1:])
    gl = gl_ref[...]
    branches = ((ya_ref, wa_ref, g0_ref, b0_ref), (yb_ref, wb_ref, g1_ref, b1_ref),
                (yc_ref, wc_ref, g2_ref, b2_ref), (yd_ref, wd_ref, g3_ref, b3_ref))
    merged = None
    for y_ref, w_ref, g_ref, b_ref in branches:
        gate = jax.nn.sigmoid(_dot(gl, g_ref[...].astype(BF16)) + b_ref[...])
        term = gate * _dot(y_ref[...], w_ref[...].astype(BF16))
        merged = term if merged is None else merged + term
    o_ref[...] = merged.astype(o_ref.dtype)


def _merge(ya, yb, yc, yd, rest, gl_col, w_a, w_b, w_c, w_d, w_gate_up, b_gate, cast_jobs,
           layer, tm=1024, tn=256):
    s = ya.shape[0]
    d = w_a.shape[2]
    rank = w_gate_up.shape[1]
    nj = d // tn

    def act(arr):
        return pl.BlockSpec((tm, arr.shape[1]), lambda i, j: (i, 0))

    def wgt(arr):
        return _layer_cols(layer, arr.shape[1], tn, lambda j: j)

    def gate_w(b):
        return _layer_cols(layer, rank, tn, lambda j: b * nj + j)

    def gate_b(b):
        return _layer_cols(layer, 1, tn, lambda j: b * nj + j)

    bias = b_gate.reshape(b_gate.shape[0], 1, N_BRANCHES * d)
    casts = _SlabCasts(cast_jobs, s // tm, nj)
    outs = pl.pallas_call(
        functools.partial(_merge_body, casts=casts),
        grid=(s // tm, nj),
        in_specs=[act(ya), act(yb), act(yc), act(yd),
                  pl.BlockSpec((tm, rank), lambda i, j: (i, gl_col)),
                  wgt(w_a), wgt(w_b), wgt(w_c), wgt(w_d),
                  gate_w(0), gate_w(1), gate_w(2), gate_w(3),
                  gate_b(0), gate_b(1), gate_b(2), gate_b(3)] + casts.in_specs,
        out_specs=[pl.BlockSpec((tm, tn), lambda i, j: (i, j))] + casts.out_specs,
        out_shape=[jax.ShapeDtypeStruct((s, d), BF16)] + casts.out_shapes,
        compiler_params=_params("parallel", "arbitrary"),
        name="merge",
    )(ya, yb, yc, yd, rest, w_a, w_b, w_c, w_d,
      w_gate_up, w_gate_up, w_gate_up, w_gate_up, bias, bias, bias, bias, *casts.operands)
    return outs[0], outs[1:]


def kernel(x, ln_ffn1, w1_gate, w1_up, w1_down, ln_mix, w_in, lambda_q1, lambda_k1, lambda_q2, lambda_k2, subln, conv_w, pool_w, pool_scale, w_gate_up, b_gate, w_branch_a, w_branch_b, w_branch_c, w_branch_d, w_out, ln_ffn2, w2_gate, w2_up, w2_down, ln_final):
    b, s, d = x.shape
    depth = ln_ffn1.shape[0]
    mix = 3 * d // 8
    tables = _rope_tables(s)

    tn_rope, tn_cols = 512, 512
    pr, pc = mix // tn_rope, mix // tn_cols
    n_rest = (w_in.shape[2] - 4 * mix) // tn_cols
    hd_per_mix = mix // HEAD_DIM
    q_scale = HEAD_DIM ** -0.5 * math.log2(math.e)

    outs = []
    for bi in range(b):
        xs = x[bi]
        normed = _prep(xs)
        ffn1_w = (_cast_layer(w1_gate, 0, ln_ffn1[0]), _cast_layer(w1_up, 0, ln_ffn1[0]))
        for l in range(depth):
            a, (w_down,) = _ffn_up(normed, *ffn1_w, [(w1_down, l, None)])
            xs, normed = _mm_residual(a, w_down, xs, 0.5)

            gain = _lane_replicated(ln_mix[l])
            qk = _mm_rope(normed, gain, w_in, l, tables,
                          lambda j: jnp.where(j < 2 * pr, j, j + pr), 4 * pr, pr, q_scale,
                          tn=tn_rope)
            rest = _mm_cols(normed, gain, w_in, l,
                            lambda j: jnp.where(j < pc, j + 2 * pc, j + 4 * pc), n_rest, tn=tn_cols)

            lam_init = 0.8 - 0.6 * math.exp(-0.3 * l)
            lam_vecs = jnp.stack([lambda_q1[l], lambda_k1[l], lambda_q2[l], lambda_k2[l]])
            ya = _diff_attn(qk, rest, lam_vecs, subln[l], lam_init,
                            q_col0=0, k_col0=DIFF_HEADS, v_col0=0)

            ahead = ([(w1_gate, l + 1, ln_ffn1[l + 1]), (w1_up, l + 1, ln_ffn1[l + 1])]
                     if l + 1 < depth else [])
            yb, ffn1_w = _dil_attn(qk, rest, q_col0=2 * hd_per_mix, k_col0=3 * hd_per_mix,
                                   v_col0=hd_per_mix, cast_jobs=ahead)

            yc = _conv_mixer(rest, conv_w, l, b_col=2, c_col=3, u_col=4, width=mix)
            yd = _pool_mixer(rest, pool_w, pool_scale[l], l, u_col=5, width=mix)

            rank = w_gate_up.shape[1]
            merged, (w_out_l, *ffn2_w) = _merge(
                ya, yb, yc, yd, rest, 6 * mix // rank, w_branch_a, w_branch_b, w_branch_c,
                w_branch_d, w_gate_up, b_gate,
                [(w_out, l, None), (w2_gate, l, ln_ffn2[l]), (w2_up, l, ln_ffn2[l])], l)
            xs, normed = _mm_residual(merged, w_out_l, xs, 1.0)

            a, (w_down,) = _ffn_up(normed, *ffn2_w, [(w2_down, l, None)])
            xs, normed = _mm_residual(a, w_down, xs, 0.5)
        outs.append(_rmsnorm(xs, ln_final, F32))
    return jnp.stack(outs, axis=0)
```
